```python
import math
import jax
import jax.numpy as jnp
from jax import lax
import numpy as np

D_MODEL = 2048
BATCH = 2
SEQ = 4096
DEPTH = 4
DEC_BATCH = 8
DEC_SEQ = 64
PAST_LEN = 2048

CHUNK = 64
N_MIXERS = 3
N_A = (DEPTH + 2) // 3
N_B = (DEPTH + 1) // 3
N_C = DEPTH // 3

A_HEADS = 16
A_HEAD_DIM = D_MODEL // (2 * A_HEADS)
A_V_DIM = 2 * A_HEAD_DIM
ROT_DIM = A_HEAD_DIM // 4
ROPE_THETA = 500000.0
Q_BLOCK = 128

POOL_WINDOWS = (2, 4, 8, 16)
POOL_GROUP_DIM = D_MODEL // len(POOL_WINDOWS)
POOL_HIST = max(POOL_WINDOWS) - 1

C_QK_HEADS = 16
C_V_HEADS = 32
C_HEAD_DIM = 128
C_KEY_DIM = C_QK_HEADS * C_HEAD_DIM
C_VAL_DIM = C_V_HEADS * C_HEAD_DIM
C_CONV_DIM = 2 * C_KEY_DIM + C_VAL_DIM
C_CONV = 4
C_IN_DIM = C_CONV_DIM + C_VAL_DIM + 2 * C_V_HEADS

MEM_LEN = 256
M_HEADS = 4
M_HEAD_DIM = D_MODEL // M_HEADS

N_EXPERTS = 32
TOP_K = 4
D_FF = D_MODEL
SWIGLU_LIMIT = 7.0
SWIGLU_ALPHA = 1.702
MOE_BLOCK = 128

DN_ALPHA = (2 * DEPTH) ** 0.25
DN_BETA = (8 * DEPTH) ** -0.25
LN_EPS = 1e-5

kernel_name = 'hybrid_streaming_encoder_step'


def layer_norm(x, g, b):
    xf = x.astype(jnp.float32)
    mu = jnp.mean(xf, -1, keepdims=True)
    var = jnp.mean(jnp.square(xf - mu), -1, keepdims=True)
    y = (xf - mu) * lax.rsqrt(var + LN_EPS) * g.astype(jnp.float32) + b.astype(jnp.float32)
    return y.astype(x.dtype)


def rms_norm(x, g, eps):
    xf = x.astype(jnp.float32)
    return xf * lax.rsqrt(jnp.mean(xf * xf, -1, keepdims=True) + eps) * g.astype(jnp.float32)


def rope_partial(x, pos):
    inv = ROPE_THETA ** (-jnp.arange(0, ROT_DIM, 2, dtype=jnp.float32) / ROT_DIM)
    ang = pos.astype(jnp.float32)[:, None] * inv[None, :]
    cos = jnp.cos(ang)[None, :, None, :]
    sin = jnp.sin(ang)[None, :, None, :]
    xr = x[..., :ROT_DIM].astype(jnp.float32)
    x1, x2 = xr[..., :ROT_DIM // 2], xr[..., ROT_DIM // 2:]
    rot = jnp.concatenate([x1 * cos - x2 * sin, x2 * cos + x1 * sin], -1)
    return jnp.concatenate([rot.astype(x.dtype), x[..., ROT_DIM:]], -1)


def diff_attn_core(q, k, v, mask, lam):
    B, nq, nk = q.shape[0], q.shape[1], k.shape[1]
    s = jnp.einsum('bqhd,bkhd->bhqk', q, k, preferred_element_type=jnp.float32) * (A_HEAD_DIM ** -0.5)
    s = jnp.where(mask[None, None], s, -jnp.inf)
    p = jax.nn.softmax(s, axis=-1).reshape(B, A_HEADS, 2, nq, nk)
    w = p[:, :, 0] - lam * p[:, :, 1]
    return jnp.einsum('bhqk,bkhe->bqhe', w.astype(v.dtype), v, preferred_element_type=jnp.float32)


def diff_attention(x, k_hist, v_hist, pos0, w_qkv, w_o, lq1, lk1, lq2, lk2, subln_g, layer_idx):
    B, L, _ = x.shape
    qkv = x @ w_qkv
    q = qkv[..., :D_MODEL].reshape(B, L, 2 * A_HEADS, A_HEAD_DIM)
    k = qkv[..., D_MODEL:2 * D_MODEL].reshape(B, L, 2 * A_HEADS, A_HEAD_DIM)
    v = qkv[..., 2 * D_MODEL:].reshape(B, L, A_HEADS, A_V_DIM)
    pos = pos0 + jnp.arange(L, dtype=jnp.int32)
    q = rope_partial(q, pos)
    k = rope_partial(k, pos)
    if k_hist is None:
        keys, vals, kpos = k, v, pos
    else:
        keys = jnp.concatenate([k_hist, k], 1)
        vals = jnp.concatenate([v_hist, v], 1)
        kpos = jnp.concatenate([jnp.arange(k_hist.shape[1], dtype=jnp.int32), pos])
    lam_init = 0.8 - 0.6 * math.exp(-0.3 * layer_idx)
    lam = (jnp.exp(jnp.sum(lq1.astype(jnp.float32) * lk1.astype(jnp.float32)))
           - jnp.exp(jnp.sum(lq2.astype(jnp.float32) * lk2.astype(jnp.float32))) + lam_init)
    kchunk = kpos // CHUNK

    def block(qb, qpos):
        mask = kchunk[None, :] <= (qpos // CHUNK)[:, None]
        return diff_attn_core(qb, keys, vals, mask, lam)

    if L > Q_BLOCK and L % Q_BLOCK == 0:
        nb = L // Q_BLOCK
        qb = q.reshape(B, nb, Q_BLOCK, 2 * A_HEADS, A_HEAD_DIM).swapaxes(0, 1)
        ob = lax.map(lambda a: block(a[0], a[1]), (qb, pos.reshape(nb, Q_BLOCK)))
        o = ob.swapaxes(0, 1).reshape(B, L, A_HEADS, A_V_DIM)
    else:
        o = block(q, pos)
    o = (rms_norm(o, subln_g, 1e-5) * (1.0 - lam_init)).astype(x.dtype)
    return o.reshape(B, L, D_MODEL) @ w_o, k, v


def pool_mixer(x, hist, pos0, w_pool, pool_scale):
    B, L, _ = x.shape
    H = hist.shape[1]
    xe = jnp.concatenate([hist, x], 1).astype(jnp.float32)
    cs = jnp.concatenate([jnp.zeros((B, 1, D_MODEL), jnp.float32), jnp.cumsum(xe, axis=1)], 1)
    pos = pos0 + jnp.arange(L, dtype=jnp.int32)
    outs = []
    for g, w in enumerate(POOL_WINDOWS):
        c0, c1 = g * POOL_GROUP_DIM, (g + 1) * POOL_GROUP_DIM
        win = cs[:, H + 1:H + 1 + L, c0:c1] - cs[:, H + 1 - w:H + 1 - w + L, c0:c1]
        cnt = jnp.minimum(pos + 1, w).astype(jnp.float32)[None, :, None]
        outs.append(win / cnt - xe[:, H:, c0:c1])
    pooled = jnp.stack(outs, 2).astype(x.dtype)
    y = jnp.einsum('blgc,gcd->blgd', pooled, w_pool).reshape(B, L, D_MODEL) * pool_scale
    return y, xe[:, -POOL_HIST:].astype(x.dtype)


def l2norm(x):
    return x * lax.rsqrt(jnp.sum(x * x, -1, keepdims=True) + 1e-6)


def chunk_gated_delta(q, k, v, g, beta, state):
    B, L, H, dk = q.shape
    pad = (-L) % CHUNK
    q = l2norm(q) * (dk ** -0.5)
    k = l2norm(k)
    if pad:
        padw = lambda t: jnp.pad(t, [(0, 0), (0, pad)] + [(0, 0)] * (t.ndim - 2))
        q, k, v, g, beta = padw(q), padw(k), padw(v), padw(g), padw(beta)
    nc = (L + pad) // CHUNK
    to_c = lambda t: t.reshape(B, nc, CHUNK, H, -1).transpose(1, 0, 3, 2, 4)
    q, k, v = to_c(q), to_c(k), to_c(v)
    g = to_c(g)[..., 0]
    beta = to_c(beta)[..., 0]
    gc = jnp.cumsum(g, -1)
    kb = k * beta[..., None]
    vb = v * beta[..., None]
    incl = jnp.tril(jnp.ones((CHUNK, CHUNK), bool))
    strict = jnp.tril(jnp.ones((CHUNK, CHUNK), bool), -1)
    diff = gc[..., :, None] - gc[..., None, :]
    decay = jnp.where(incl, jnp.exp(jnp.where(incl, diff, 0.0)), 0.0)
    eye = jnp.eye(CHUNK, dtype=jnp.float32)
    a_low = jnp.where(strict, jnp.einsum('nbhid,nbhjd->nbhij', kb, k) * decay, 0.0)
    t_inv = lax.linalg.triangular_solve(eye + a_low, jnp.broadcast_to(eye, a_low.shape),
                                        left_side=True, lower=True, unit_diagonal=True)
    u = t_inv @ vb
    w = t_inv @ (kb * jnp.exp(gc)[..., None])
    a_qk = jnp.where(incl, jnp.einsum('nbhid,nbhjd->nbhij', q, k) * decay, 0.0)

    def step(s, inp):
        qi, ki, ui, wi, gi, ai = inp
        v_new = ui - wi @ s
        o = (qi * jnp.exp(gi)[..., None]) @ s + ai @ v_new
        gl = gi[..., -1]
        s = s * jnp.exp(gl)[..., None, None] + jnp.einsum(
            'bhcd,bhce->bhde', ki * jnp.exp(gl[..., None] - gi)[..., None], v_new)
        return s, o

    state, o = lax.scan(step, state, (q, k, u, w, gc, a_qk))
    o = o.transpose(1, 0, 3, 2, 4).reshape(B, nc * CHUNK, H, -1)[:, :L]
    return o, state


def gated_deltanet(x, conv_hist, state, w_in, conv_w, a_log, dt_bias, norm_g, w_o):
    B, L, _ = x.shape
    proj = x @ w_in
    mixed = proj[..., :C_CONV_DIM]
    z = proj[..., C_CONV_DIM:C_CONV_DIM + C_VAL_DIM].reshape(B, L, C_V_HEADS, C_HEAD_DIM)
    b_in = proj[..., C_CONV_DIM + C_VAL_DIM:C_CONV_DIM + C_VAL_DIM + C_V_HEADS].astype(jnp.float32)
    a_in = proj[..., C_CONV_DIM + C_VAL_DIM + C_V_HEADS:].astype(jnp.float32)
    xe = jnp.concatenate([conv_hist, mixed], 1)
    conv = xe[:, 0:L] * conv_w[0]
    for j in range(1, C_CONV):
        conv = conv + xe[:, j:j + L] * conv_w[j]
    conv = jax.nn.silu(conv.astype(jnp.float32))
    q = conv[..., :C_KEY_DIM].reshape(B, L, C_QK_HEADS, C_HEAD_DIM)
    k = conv[..., C_KEY_DIM:2 * C_KEY_DIM].reshape(B, L, C_QK_HEADS, C_HEAD_DIM)
    v = conv[..., 2 * C_KEY_DIM:].reshape(B, L, C_V_HEADS, C_HEAD_DIM)
    rep = C_V_HEADS // C_QK_HEADS
    q = jnp.repeat(q, rep, axis=2)
    k = jnp.repeat(k, rep, axis=2)
    beta = jax.nn.sigmoid(b_in)
    g = -jnp.exp(a_log.astype(jnp.float32)) * jax.nn.softplus(a_in + dt_bias.astype(jnp.float32))
    o, new_state = chunk_gated_delta(q, k, v, g, beta, state.astype(jnp.float32))
    o = rms_norm(o, norm_g, 1e-6) * jax.nn.silu(z.astype(jnp.float32))
    y = o.astype(x.dtype).reshape(B, L, C_VAL_DIM) @ w_o
    return y, xe[:, -(C_CONV - 1):], new_state.astype(state.dtype)


def mem_kv(mem, w_kv):
    B, M, _ = mem.shape
    kv = mem @ w_kv
    return (kv[..., :D_MODEL].reshape(B, M, M_HEADS, M_HEAD_DIM),
            kv[..., D_MODEL:].reshape(B, M, M_HEADS, M_HEAD_DIM))


def mem_cross_attn(x, mk, mv, w_q, w_o):
    B, L, _ = x.shape
    q = (x @ w_q).reshape(B, L, M_HEADS, M_HEAD_DIM)
    s = jnp.einsum('blhd,bmhd->bhlm', q, mk, preferred_element_type=jnp.float32) * (M_HEAD_DIM ** -0.5)
    p = jax.nn.softmax(s, axis=-1)
    o = jnp.einsum('bhlm,bmhd->blhd', p.astype(mv.dtype), mv)
    return o.reshape(B, L, D_MODEL) @ w_o


def moe(x, w_router, b_router, w_gu, b_gu, w_down, b_down):
    B, L, _ = x.shape
    T = B * L
    xt = x.reshape(T, D_MODEL)
    logits = jnp.dot(xt, w_router, preferred_element_type=jnp.float32) + b_router.astype(jnp.float32)
    top_v, top_i = lax.top_k(logits, TOP_K)
    gates = jax.nn.softmax(top_v, axis=-1)
    n_as = T * TOP_K
    e_flat = top_i.reshape(n_as)
    tok_flat = jnp.arange(n_as, dtype=jnp.int32) // TOP_K
    order = jnp.argsort(e_flat)
    e_sorted = e_flat[order]
    counts = jnp.zeros((N_EXPERTS,), jnp.int32).at[e_flat].add(1)
    start = jnp.cumsum(counts) - counts
    padded = (counts + MOE_BLOCK - 1) // MOE_BLOCK * MOE_BLOCK
    pad_end = jnp.cumsum(padded)
    pad_start = pad_end - padded
    dest = pad_start[e_sorted] + jnp.arange(n_as, dtype=jnp.int32) - start[e_sorted]
    n_blocks = -(-n_as // MOE_BLOCK) + N_EXPERTS
    n_rows = n_blocks * MOE_BLOCK
    row_tok = jnp.zeros((n_rows,), jnp.int32).at[dest].set(tok_flat[order])
    row_gate = jnp.zeros((n_rows,), jnp.float32).at[dest].set(gates.reshape(n_as)[order])
    blk_exp = jnp.minimum(jnp.searchsorted(pad_end, jnp.arange(n_blocks, dtype=jnp.int32) * MOE_BLOCK,
                                           side='right'), N_EXPERTS - 1)
    xs = xt[row_tok].reshape(n_blocks, MOE_BLOCK, D_MODEL)

    def expert_block(args):
        xb, e = args
        h = (xb @ w_gu[e] + b_gu[e]).astype(jnp.float32)
        gate = jnp.minimum(h[:, :D_FF], SWIGLU_LIMIT)
        up = jnp.clip(h[:, D_FF:], -SWIGLU_LIMIT, SWIGLU_LIMIT)
        act = (up + 1.0) * gate * jax.nn.sigmoid(SWIGLU_ALPHA * gate)
        return act.astype(xb.dtype) @ w_down[e] + b_down[e]

    ys = lax.map(expert_block, (xs, blk_exp)).reshape(n_rows, D_MODEL)
    out = jax.ops.segment_sum(ys.astype(jnp.float32) * row_gate[:, None], row_tok, num_segments=T)
    return out.astype(x.dtype).reshape(B, L, D_MODEL)


def run_group(x, pos0, a_k_hist, a_v_hist, pool_hist, conv_hist, delta_state, mem_k, mem_v, P):
    new_k, new_v, new_pool, new_conv, new_delta = [], [], [], [], []
    for i in range(DEPTH):
        m, j = i % N_MIXERS, i // N_MIXERS
        if m == 0:
            h, kn, vn = diff_attention(
                x, None if a_k_hist is None else a_k_hist[j], None if a_v_hist is None else a_v_hist[j],
                pos0, P['w_qkv_a'][j], P['w_o_a'][j], P['lam_q1'][j], P['lam_k1'][j],
                P['lam_q2'][j], P['lam_k2'][j], P['subln_g'][j], i)
            new_k.append(kn)
            new_v.append(vn)
        elif m == 1:
            h, ph = pool_mixer(x, pool_hist[j], pos0, P['w_pool'][j], P['pool_scale'][j])
            new_pool.append(ph)
        else:
            h, ch, st = gated_deltanet(x, conv_hist[j], delta_state[j], P['w_in_c'][j], P['conv_w_c'][j],
                                       P['a_log_c'][j], P['dt_bias_c'][j], P['norm_g_c'][j], P['w_o_c'][j])
            new_conv.append(ch)
            new_delta.append(st)
        x = layer_norm(DN_ALPHA * x + h, P['ln_g'][i, 0], P['ln_b'][i, 0])
        c = mem_cross_attn(x, mem_k[i], mem_v[i], P['w_q_m'][i], P['w_o_m'][i])
        x = layer_norm(DN_ALPHA * x + c, P['ln_g'][i, 1], P['ln_b'][i, 1])
        f = moe(x, P['w_router'][i], P['b_router'][i], P['w_gu'][i], P['b_gu'][i], P['w_down'][i], P['b_down'][i])
        x = layer_norm(DN_ALPHA * x + f, P['ln_g'][i, 2], P['ln_b'][i, 2])
    return (x, jnp.stack(new_k), jnp.stack(new_v), jnp.stack(new_pool),
            jnp.stack(new_conv), jnp.stack(new_delta))


def setup_inputs(seed: int = 0) -> dict:
    key = jax.random.key(seed)
    ks = iter(jax.random.split(key, 40))
    f32 = jnp.float32
    d = D_MODEL
    nrm = lambda shape, scale: jax.random.normal(next(ks), shape, f32) * scale
    inp = {}
    inp['x_prompt'] = nrm((BATCH, SEQ, d), 1.0)
    inp['x_sample'] = nrm((DEC_BATCH, DEC_SEQ, d), 1.0)
    inp['cache_a_k'] = nrm((N_A, DEC_BATCH, PAST_LEN, 2 * A_HEADS, A_HEAD_DIM), 1.0)
    inp['cache_a_v'] = nrm((N_A, DEC_BATCH, PAST_LEN, A_HEADS, A_V_DIM), 1.0)
    inp['cache_pool'] = nrm((N_B, DEC_BATCH, POOL_HIST, d), 1.0)
    inp['cache_conv'] = nrm((N_C, DEC_BATCH, C_CONV - 1, C_CONV_DIM), 1.0)
    inp['state_delta'] = nrm((N_C, DEC_BATCH, C_V_HEADS, C_HEAD_DIM, C_HEAD_DIM), 0.1)
    inp['cache_mem_k'] = nrm((DEPTH, DEC_BATCH, MEM_LEN, M_HEADS, M_HEAD_DIM), 1.0)
    inp['cache_mem_v'] = nrm((DEPTH, DEC_BATCH, MEM_LEN, M_HEADS, M_HEAD_DIM), 1.0)
    inp['mem_prompt'] = nrm((BATCH, MEM_LEN, d), 1.0)
    inp['ln_g'] = 1.0 + nrm((DEPTH, 3, d), 0.02)
    inp['ln_b'] = nrm((DEPTH, 3, d), 0.01)
    inp['w_qkv_a'] = nrm((N_A, d, 3 * d), d ** -0.5)
    inp['w_o_a'] = nrm((N_A, d, d), d ** -0.5 * DN_BETA)
    inp['lam_q1'] = nrm((N_A, A_HEAD_DIM), 0.1)
    inp['lam_k1'] = nrm((N_A, A_HEAD_DIM), 0.1)
    inp['lam_q2'] = nrm((N_A, A_HEAD_DIM), 0.1)
    inp['lam_k2'] = nrm((N_A, A_HEAD_DIM), 0.1)
    inp['subln_g'] = 1.0 + nrm((N_A, A_V_DIM), 0.02)
    inp['w_pool'] = nrm((N_B, len(POOL_WINDOWS), POOL_GROUP_DIM, POOL_GROUP_DIM), POOL_GROUP_DIM ** -0.5 * DN_BETA)
    inp['pool_scale'] = 1.0 + nrm((N_B, d), 0.02)
    inp['w_in_c'] = nrm((N_C, d, C_IN_DIM), d ** -0.5)
    inp['conv_w_c'] = nrm((N_C, C_CONV, C_CONV_DIM), C_CONV ** -0.5)
    inp['a_log_c'] = jnp.log(jax.random.uniform(next(ks), (N_C, C_V_HEADS), f32, 0.02, 0.5))
    inp['dt_bias_c'] = nrm((N_C, C_V_HEADS), 0.1) - 2.0
    inp['norm_g_c'] = 1.0 + nrm((N_C, C_HEAD_DIM), 0.02)
    inp['w_o_c'] = nrm((N_C, C_VAL_DIM, d), C_VAL_DIM ** -0.5 * DN_BETA)
    inp['w_q_m'] = nrm((DEPTH, d, d), d ** -0.5)
    inp['w_kv_m'] = nrm((DEPTH, d, 2 * d), d ** -0.5)
    inp['w_o_m'] = nrm((DEPTH, d, d), d ** -0.5 * DN_BETA)
    inp['w_router'] = nrm((DEPTH, d, N_EXPERTS), d ** -0.5)
    inp['b_router'] = nrm((DEPTH, N_EXPERTS), 0.01)
    inp['w_gu'] = nrm((DEPTH, N_EXPERTS, d, 2 * D_FF), d ** -0.5)
    inp['b_gu'] = nrm((DEPTH, N_EXPERTS, 2 * D_FF), 0.01)
    inp['w_down'] = nrm((DEPTH, N_EXPERTS, D_FF, d), D_FF ** -0.5 * DN_BETA)
    inp['b_down'] = nrm((DEPTH, N_EXPERTS, d), 0.01)
    return inp


def reference(x_prompt, x_sample, cache_a_k, cache_a_v, cache_pool, cache_conv, state_delta,
              cache_mem_k, cache_mem_v, mem_prompt, ln_g, ln_b, w_qkv_a, w_o_a, lam_q1, lam_k1,
              lam_q2, lam_k2, subln_g, w_pool, pool_scale, w_in_c, conv_w_c, a_log_c, dt_bias_c,
              norm_g_c, w_o_c, w_q_m, w_kv_m, w_o_m, w_router, b_router, w_gu, b_gu, w_down, b_down):
    P = dict(ln_g=ln_g, ln_b=ln_b, w_qkv_a=w_qkv_a, w_o_a=w_o_a, lam_q1=lam_q1, lam_k1=lam_k1,
             lam_q2=lam_q2, lam_k2=lam_k2, subln_g=subln_g, w_pool=w_pool, pool_scale=pool_scale,
             w_in_c=w_in_c, conv_w_c=conv_w_c, a_log_c=a_log_c, dt_bias_c=dt_bias_c, norm_g_c=norm_g_c,
             w_o_c=w_o_c, w_q_m=w_q_m, w_o_m=w_o_m, w_router=w_router, b_router=b_router,
             w_gu=w_gu, b_gu=b_gu, w_down=w_down, b_down=b_down)
    mk_list, mv_list = [], []
    for i in range(DEPTH):
        mk, mv = mem_kv(mem_prompt, w_kv_m[i])
        mk_list.append(mk)
        mv_list.append(mv)
    mem_k_prompt = jnp.stack(mk_list)
    mem_v_prompt = jnp.stack(mv_list)
    B = x_prompt.shape[0]
    dt = x_prompt.dtype
    y_prompt, a_k_prompt, a_v_prompt, pool_prompt, conv_prompt, delta_prompt = run_group(
        x_prompt, 0, None, None,
        jnp.zeros((N_B, B, POOL_HIST, D_MODEL), dt),
        jnp.zeros((N_C, B, C_CONV - 1, C_CONV_DIM), dt),
        jnp.zeros((N_C, B, C_V_HEADS, C_HEAD_DIM, C_HEAD_DIM), dt),
        mem_k_prompt, mem_v_prompt, P)
    y_sample, a_k_sample, a_v_sample, pool_sample, conv_sample, delta_sample = run_group(
        x_sample, PAST_LEN, cache_a_k, cache_a_v, cache_pool, cache_conv, state_delta,
        cache_mem_k, cache_mem_v, P)
    return (y_prompt, y_sample, a_k_prompt, a_v_prompt, a_k_sample, a_v_sample,
            pool_prompt, pool_sample, conv_prompt, conv_sample, delta_prompt, delta_sample,
            mem_k_prompt, mem_v_prompt)
```

```python
import functools
import math

import jax
import jax.numpy as jnp
from jax import lax
from jax.experimental import pallas as pl
from jax.experimental.pallas import tpu as pltpu

F32 = jnp.float32
BF16 = jnp.bfloat16

D_MODEL = 2048
DEPTH = 4
CHUNK = 64
N_MIXERS = 3
A_HEADS = 16
A_HEAD_DIM = D_MODEL // (2 * A_HEADS)
A_V_DIM = 2 * A_HEAD_DIM
ROT_DIM = A_HEAD_DIM // 4
ROPE_THETA = 500000.0
POOL_WINDOWS = (2, 4, 8, 16)
POOL_GROUP_DIM = D_MODEL // len(POOL_WINDOWS)
POOL_HIST = max(POOL_WINDOWS) - 1
C_QK_HEADS = 16
C_V_HEADS = 32
C_HEAD_DIM = 128
C_KEY_DIM = C_QK_HEADS * C_HEAD_DIM
C_VAL_DIM = C_V_HEADS * C_HEAD_DIM
C_CONV_DIM = 2 * C_KEY_DIM + C_VAL_DIM
C_CONV = 4
M_HEADS = 4
M_HEAD_DIM = D_MODEL // M_HEADS
N_EXPERTS = 32
TOP_K = 4
D_FF = D_MODEL
SWIGLU_LIMIT = 7.0
SWIGLU_ALPHA = 1.702
DN_ALPHA = (2 * DEPTH) ** 0.25
LN_EPS = 1e-5

LANES = 128
SUBLANES = 8
VMEM_LIMIT = 52 * 1024 * 1024
HEAD_GROUP = 8
MOE_TILE = 256


def _cparams(sem):
    return pltpu.CompilerParams(dimension_semantics=sem, vmem_limit_bytes=VMEM_LIMIT)


def _tile(n, pref, mult=SUBLANES):
    t = min(pref, n)
    while t > mult and (n % t or t % mult):
        t -= mult
    assert n % t == 0, (n, pref)
    return t


def _dot(a, b):
    return jnp.dot(a, b, preferred_element_type=F32)


def _dot_nt(a, b):
    return lax.dot_general(a, b, (((1,), (1,)), ((), ())), preferred_element_type=F32)


def _dot_tn(a, b):
    return lax.dot_general(a, b, (((0,), (0,)), ((), ())), preferred_element_type=F32)


def _split(a):
    hi = a.astype(BF16)
    lo = (a - hi.astype(F32)).astype(BF16)
    return hi, lo


def _dot3(a, b):
    ah, al = _split(a)
    bh, bl = _split(b)
    return _dot(ah, bh) + _dot(ah, bl) + _dot(al, bh)


def _sigmoid(x):
    return 1.0 / (1.0 + jnp.exp(-x))


def _layer_norm(y, g, b):
    mu = jnp.mean(y, axis=-1, keepdims=True)
    d = y - mu
    var = jnp.mean(d * d, axis=-1, keepdims=True)
    return d * lax.rsqrt(var + LN_EPS) * g + b


def _mm_kernel(a_ref, w_ref, o_ref):
    o_ref[...] = _dot(a_ref[...], w_ref[...]).astype(o_ref.dtype)


def matmul(a, w, out_dtype=F32, tm=512, tn=1024):
    M, K = a.shape
    N = w.shape[1]
    tm, tn = _tile(M, tm), _tile(N, tn, LANES)
    return pl.pallas_call(
        _mm_kernel,
        grid=(N // tn, M // tm),
        in_specs=[pl.BlockSpec((tm, K), lambda j, i: (i, 0)),
                  pl.BlockSpec((K, tn), lambda j, i: (0, j))],
        out_specs=pl.BlockSpec((tm, tn), lambda j, i: (i, j)),
        out_shape=jax.ShapeDtypeStruct((M, N), out_dtype),
        compiler_params=_cparams(("parallel", "parallel")),
        name="matmul",
    )(a, w)


def _mm_rope_kernel(a_ref, w_ref, c_ref, s1_ref, s2_ref, o_ref, *, n_rope, tn):
    j = pl.program_id(0)
    acc = _dot(a_ref[...], w_ref[...])

    @pl.when(j < n_rope)
    def _():
        c, s1, s2 = c_ref[...], s1_ref[...], s2_ref[...]
        half = ROT_DIM // 2
        for cb in range(tn // LANES):
            x = acc[:, cb * LANES:(cb + 1) * LANES]
            o_ref[:, cb * LANES:(cb + 1) * LANES] = (
                x * c + pltpu.roll(x, half, 1) * s1 + pltpu.roll(x, LANES - half, 1) * s2)

    @pl.when(j >= n_rope)
    def _():
        o_ref[...] = acc


def matmul_rope(a, w, tabs, n_rope_cols, tm=512, tn=1024):
    M, K = a.shape
    N = w.shape[1]
    tm, tn = _tile(M, tm), _tile(N, tn, LANES)
    tab_spec = pl.BlockSpec((tm, LANES), lambda j, i: (i, 0))
    return pl.pallas_call(
        functools.partial(_mm_rope_kernel, n_rope=n_rope_cols // tn, tn=tn),
        grid=(N // tn, M // tm),
        in_specs=[pl.BlockSpec((tm, K), lambda j, i: (i, 0)),
                  pl.BlockSpec((K, tn), lambda j, i: (0, j)),
                  tab_spec, tab_spec, tab_spec],
        out_specs=pl.BlockSpec((tm, tn), lambda j, i: (i, j)),
        out_shape=jax.ShapeDtypeStruct((M, N), F32),
        compiler_params=_cparams(("parallel", "parallel")),
        name="matmul_rope",
    )(a, w, *tabs)


def rope_tables(pos):
    inv = ROPE_THETA ** (-jnp.arange(0, ROT_DIM, 2, dtype=F32) / ROT_DIM)
    ang = pos.astype(F32)[:, None] * inv[None, :]
    cos, sin = jnp.cos(ang), jnp.sin(ang)
    half = ROT_DIM // 2
    ones = jnp.ones((pos.shape[0], A_HEAD_DIM - ROT_DIM), F32)
    zeros = jnp.zeros((pos.shape[0], A_HEAD_DIM - ROT_DIM), F32)
    zh = jnp.zeros((pos.shape[0], half), F32)
    c = jnp.concatenate([cos, cos, ones], 1)
    s1 = jnp.concatenate([zh, sin, zeros], 1)
    s2 = jnp.concatenate([-sin, zh, zeros], 1)
    rep = LANES // A_HEAD_DIM
    return tuple(jnp.tile(t, (1, rep)) for t in (c, s1, s2))


def _mm_res_ln_kernel(a_ref, w_ref, x_ref, g_ref, b_ref, of_ref, ob_ref, acc_ref, *, nk):
    k = pl.program_id(1)

    def finish(h):
        y = _layer_norm(DN_ALPHA * x_ref[...] + h, g_ref[...], b_ref[...])
        of_ref[...] = y
        ob_ref[...] = y.astype(BF16)

    part = _dot(a_ref[...], w_ref[...])
    if nk == 1:
        finish(part)
    else:
        @pl.when(k == 0)
        def _():
            acc_ref[...] = part

        @pl.when(jnp.logical_and(k > 0, k < nk - 1))
        def _():
            acc_ref[...] += part

        @pl.when(k == nk - 1)
        def _():
            finish(acc_ref[...] + part)


def matmul_res_ln(a, w, x, g, b, tm=256, tk=2048):
    M, K = a.shape
    N = w.shape[1]
    tm, tk = _tile(M, tm), _tile(K, tk, LANES)
    nk = K // tk
    vec = pl.BlockSpec((1, N), lambda i, k: (0, 0))
    row = pl.BlockSpec((tm, N), lambda i, k: (i, 0))
    return pl.pallas_call(
        functools.partial(_mm_res_ln_kernel, nk=nk),
        grid=(M // tm, nk),
        in_specs=[pl.BlockSpec((tm, tk), lambda i, k: (i, k)),
                  pl.BlockSpec((tk, N), lambda i, k: (k, 0)),
                  row, vec, vec],
        out_specs=[row, row],
        out_shape=[jax.ShapeDtypeStruct((M, N), F32), jax.ShapeDtypeStruct((M, N), BF16)],
        scratch_shapes=[pltpu.VMEM((tm, N), F32)],
        compiler_params=_cparams(("parallel", "arbitrary")),
        name="matmul_res_ln",
    )(a, w, x, g.reshape(1, N), b.reshape(1, N))


def _attn_init(m_ref, l_ref, acc_ref):
    m_ref[...] = jnp.full(m_ref.shape, -jnp.inf, F32)
    l_ref[...] = jnp.zeros(l_ref.shape, F32)
    acc_ref[...] = jnp.zeros(acc_ref.shape, F32)


def _attn_step(q_ref, k_ref, v_ref, m_ref, l_ref, acc_ref, n_heads, mask):
    tq = q_ref.shape[0]
    lane = lax.broadcasted_iota(jnp.int32, (tq, LANES), 1)
    first = lane < A_HEAD_DIM
    for h in range(n_heads):
        sl = slice(h * LANES, (h + 1) * LANES)
        q = q_ref[:, sl] * (A_HEAD_DIM ** -0.5)
        k = k_ref[:, sl].astype(BF16)
        v = v_ref[:, sl].astype(BF16)
        for mp in range(2):
            qm = jnp.where(first if mp == 0 else jnp.logical_not(first), q, 0.0).astype(BF16)
            s = _dot_nt(qm, k)
            if mask is not None:
                s = jnp.where(mask, s, -jnp.inf)
            idx = 2 * h + mp
            m_prev = m_ref[idx]
            m_new = jnp.maximum(m_prev, jnp.max(s, axis=1, keepdims=True))
            alpha = jnp.exp(m_prev - m_new)
            p = jnp.exp(s - m_new[:, :1])
            l_ref[idx] = alpha * l_ref[idx] + jnp.sum(p, axis=1, keepdims=True)
            acc_ref[idx] = alpha * acc_ref[idx] + _dot(p.astype(BF16), v)
            m_ref[idx] = m_new


def _attn_finish(lamv_ref, g_ref, o_ref, l_ref, acc_ref, n_heads, lam_init):
    lamv = lamv_ref[...]
    lam = (jnp.exp(jnp.sum(lamv[0:1] * lamv[1:2], axis=1, keepdims=True))
           - jnp.exp(jnp.sum(lamv[2:3] * lamv[3:4], axis=1, keepdims=True)) + lam_init)
    for h in range(n_heads):
        o = acc_ref[2 * h] / l_ref[2 * h] - lam * (acc_ref[2 * h + 1] / l_ref[2 * h + 1])
        ms = jnp.mean(o * o, axis=1, keepdims=True)
        y = o * lax.rsqrt(ms + 1e-5) * g_ref[...] * (1.0 - lam_init)
        o_ref[:, h * LANES:(h + 1) * LANES] = y.astype(o_ref.dtype)


def _flash_prompt_kernel(qt_ref, kt_ref, q_ref, k_ref, v_ref, lamv_ref, g_ref, o_ref,
                         m_ref, l_ref, acc_ref, *, hpb, lam_init):
    p = pl.program_id(2)
    qb, kb = qt_ref[p], kt_ref[p]
    tq = q_ref.shape[0]

    @pl.when(kb == 0)
    def _():
        _attn_init(m_ref, l_ref, acc_ref)

    @pl.when(kb < qb)
    def _():
        _attn_step(q_ref, k_ref, v_ref, m_ref, l_ref, acc_ref, hpb, None)

    @pl.when(kb == qb)
    def _():
        r = lax.broadcasted_iota(jnp.int32, (tq, tq), 0) // CHUNK
        c = lax.broadcasted_iota(jnp.int32, (tq, tq), 1) // CHUNK
        _attn_step(q_ref, k_ref, v_ref, m_ref, l_ref, acc_ref, hpb, r >= c)
        _attn_finish(lamv_ref, g_ref, o_ref, l_ref, acc_ref, hpb, lam_init)


def flash_prompt(qkv, B, L, lamv, g, lam_init, tq=512, hpb=1):
    tq = _tile(L, tq, CHUNK)
    nq = L // tq
    pairs = [(qb, kb) for qb in range(nq) for kb in range(qb + 1)]
    qt = jnp.asarray([p[0] for p in pairs], jnp.int32)
    kt = jnp.asarray([p[1] for p in pairs], jnp.int32)
    w = hpb * LANES
    hb = D_MODEL // w
    grid_spec = pltpu.PrefetchScalarGridSpec(
        num_scalar_prefetch=2,
        grid=(B, A_HEADS // hpb, len(pairs)),
        in_specs=[pl.BlockSpec((tq, w), lambda b, h, p, qt, kt: (b * nq + qt[p], h)),
                  pl.BlockSpec((tq, w), lambda b, h, p, qt, kt: (b * nq + kt[p], hb + h)),
                  pl.BlockSpec((tq, w), lambda b, h, p, qt, kt: (b * nq + kt[p], 2 * hb + h)),
                  pl.BlockSpec((4, A_HEAD_DIM), lambda b, h, p, qt, kt: (0, 0)),
                  pl.BlockSpec((1, LANES), lambda b, h, p, qt, kt: (0, 0))],
        out_specs=pl.BlockSpec((tq, w), lambda b, h, p, qt, kt: (b * nq + qt[p], h)),
        scratch_shapes=[pltpu.VMEM((2 * hpb, tq, LANES), F32)] * 3)
    return pl.pallas_call(
        functools.partial(_flash_prompt_kernel, hpb=hpb, lam_init=lam_init),
        grid_spec=grid_spec,
        out_shape=jax.ShapeDtypeStruct((B * L, D_MODEL), BF16),
        compiler_params=_cparams(("parallel", "parallel", "arbitrary")),
        name="flash_prompt",
    )(qt, kt, qkv, qkv, qkv, lamv, g.reshape(1, LANES))


def _flash_sample_kernel(q_ref, kh_ref, vh_ref, kn_ref, vn_ref, lamv_ref, g_ref, o_ref,
                         m_ref, l_ref, acc_ref, *, nhist, lam_init):
    kb = pl.program_id(1)

    @pl.when(kb == 0)
    def _():
        _attn_init(m_ref, l_ref, acc_ref)

    @pl.when(kb < nhist)
    def _():
        _attn_step(q_ref, kh_ref, vh_ref, m_ref, l_ref, acc_ref, A_HEADS, None)

    @pl.when(kb == nhist)
    def _():
        _attn_step(q_ref, kn_ref, vn_ref, m_ref, l_ref, acc_ref, A_HEADS, None)
        _attn_finish(lamv_ref, g_ref, o_ref, l_ref, acc_ref, A_HEADS, lam_init)


def flash_sample(qkv, row0, B, L, k_cache, v_cache, j, lamv, g, lam_init, tk=512):
    past = k_cache.shape[2]
    assert L == CHUNK and past % CHUNK == 0 and row0 % L == 0
    tk = _tile(past, tk)
    nhist = past // tk
    rb = row0 // L
    new = lambda c: pl.BlockSpec((L, D_MODEL), lambda b, kb: (rb + b, c))
    hist = pl.BlockSpec((None, None, tk, D_MODEL), lambda b, kb: (j, b, jnp.minimum(kb, nhist - 1), 0))
    return pl.pallas_call(
        functools.partial(_flash_sample_kernel, nhist=nhist, lam_init=lam_init),
        grid=(B, nhist + 1),
        in_specs=[new(0), hist, hist, new(1), new(2),
                  pl.BlockSpec((4, A_HEAD_DIM), lambda b, kb: (0, 0)),
                  pl.BlockSpec((1, LANES), lambda b, kb: (0, 0))],
        out_specs=pl.BlockSpec((L, D_MODEL), lambda b, kb: (b, 0)),
        out_shape=jax.ShapeDtypeStruct((B * L, D_MODEL), BF16),
        scratch_shapes=[pltpu.VMEM((2 * A_HEADS, L, LANES), F32)] * 3,
        compiler_params=_cparams(("parallel", "arbitrary")),
        name="flash_sample",
    )(qkv, k_cache, v_cache, qkv, qkv, lamv, g.reshape(1, LANES))


def _mem_attn_kernel(x_ref, wq_ref, mk_ref, mv_ref, o_ref):
    q = _dot(x_ref[...], wq_ref[...])
    for h in range(M_HEADS):
        sl = slice(h * M_HEAD_DIM, (h + 1) * M_HEAD_DIM)
        s = _dot_nt(q[:, sl].astype(BF16), mk_ref[:, sl].astype(BF16)) * (M_HEAD_DIM ** -0.5)
        p = jnp.exp(s - jnp.max(s, axis=1, keepdims=True))
        l = jnp.sum(p, axis=1, keepdims=True)
        o = _dot(p.astype(BF16), mv_ref[:, sl].astype(BF16)) / l
        o_ref[:, sl] = o.astype(o_ref.dtype)


def mem_attn(xb, wq, row0, B, L, k_arr, v_arr, k_spec, v_spec, tm=512):
    tm = _tile(L, tm)
    nb = L // tm
    rb = row0 // tm
    assert row0 % tm == 0
    return pl.pallas_call(
        _mem_attn_kernel,
        grid=(B, nb),
        in_specs=[pl.BlockSpec((tm, D_MODEL), lambda b, i: (rb + b * nb + i, 0)),
                  pl.BlockSpec((D_MODEL, D_MODEL), lambda b, i: (0, 0)),
                  k_spec, v_spec],
        out_specs=pl.BlockSpec((tm, D_MODEL), lambda b, i: (b * nb + i, 0)),
        out_shape=jax.ShapeDtypeStruct((B * L, D_MODEL), BF16),
        compiler_params=_cparams(("parallel", "parallel")),
        name="mem_attn",
    )(xb, wq, k_arr, v_arr)


def _pool_kernel(x_ref, prev_ref, hist_ref, w_ref, sc_ref, g_ref, b_ref, of_ref, ob_ref,
                 cat_ref, y_ref, *, tr, pos0):
    i = pl.program_id(1)
    H = POOL_HIST + 1
    cat_ref[0:H, :] = jnp.where(i == 0, hist_ref[...], prev_ref[...])
    cat_ref[H:, :] = x_ref[...]
    pos = pos0 + i * tr + lax.broadcasted_iota(jnp.int32, (tr, 1), 0)
    for gi, w in enumerate(POOL_WINDOWS):
        cs = slice(gi * POOL_GROUP_DIM, (gi + 1) * POOL_GROUP_DIM)
        cur = x_ref[:, cs]
        win = cur
        for jj in range(1, w):
            win = win + cat_ref[H - jj:H - jj + tr, cs]
        cnt = jnp.minimum(pos + 1, w).astype(F32)
        pooled = win / cnt - cur
        y_ref[:, cs] = _dot(pooled.astype(BF16), w_ref[gi]) * sc_ref[:, cs]
    y = _layer_norm(DN_ALPHA * x_ref[...] + y_ref[...], g_ref[...], b_ref[...])
    of_ref[...] = y
    ob_ref[...] = y.astype(BF16)


def pool_layer(x, row0, B, L, pos0, hist, w_pool_b, pool_scale, g, b, tr=256):
    tr = _tile(L, tr, POOL_HIST + 1)
    H = POOL_HIST + 1
    nb = L // tr
    assert row0 % tr == 0
    hist_p = jnp.concatenate([jnp.zeros((B, 1, D_MODEL), F32), hist], 1)
    vec = pl.BlockSpec((1, D_MODEL), lambda b_, i: (0, 0))
    row = pl.BlockSpec((tr, D_MODEL), lambda b_, i: (b_ * nb + i, 0))
    return pl.pallas_call(
        functools.partial(_pool_kernel, tr=tr, pos0=pos0),
        grid=(B, nb),
        in_specs=[pl.BlockSpec((tr, D_MODEL), lambda b_, i: (row0 // tr + b_ * nb + i, 0)),
                  pl.BlockSpec((H, D_MODEL),
                               lambda b_, i: (jnp.maximum((row0 + b_ * L + i * tr) // H - 1, 0), 0)),
                  pl.BlockSpec((None, H, D_MODEL), lambda b_, i: (b_, 0, 0)),
                  pl.BlockSpec((len(POOL_WINDOWS), POOL_GROUP_DIM, POOL_GROUP_DIM), lambda b_, i: (0, 0, 0)),
                  vec, vec, vec],
        out_specs=[row, row],
        out_shape=[jax.ShapeDtypeStruct((B * L, D_MODEL), F32),
                   jax.ShapeDtypeStruct((B * L, D_MODEL), BF16)],
        scratch_shapes=[pltpu.VMEM((tr + H, D_MODEL), F32), pltpu.VMEM((tr, D_MODEL), F32)],
        compiler_params=_cparams(("parallel", "parallel")),
        name="pool_layer",
    )(x, x, hist_p, w_pool_b, pool_scale.reshape(1, -1), g.reshape(1, -1), b.reshape(1, -1))


def _conv_kernel(cur_ref, prev_ref, hist_ref, w_ref, o_ref, cat_ref, *, tr, tc, nq, nqk):
    i = pl.program_id(1)
    c = pl.program_id(2)
    cat_ref[0:SUBLANES, :] = jnp.where(i == 0, hist_ref[...], prev_ref[...])
    cat_ref[SUBLANES:, :] = cur_ref[...]
    acc = cur_ref[...] * w_ref[C_CONV - 1:C_CONV, :]
    for jj in range(C_CONV - 1):
        s = SUBLANES - (C_CONV - 1) + jj
        acc = acc + cat_ref[s:s + tr, :] * w_ref[jj:jj + 1, :]
    y = acc * _sigmoid(acc)

    @pl.when(c < nqk)
    def _():
        scale = jnp.where(c < nq, C_HEAD_DIM ** -0.5, 1.0)
        for h in range(tc // C_HEAD_DIM):
            sl = slice(h * C_HEAD_DIM, (h + 1) * C_HEAD_DIM)
            yh = y[:, sl]
            o_ref[:, sl] = yh * (lax.rsqrt(jnp.sum(yh * yh, axis=1, keepdims=True) + 1e-6) * scale)

    @pl.when(c >= nqk)
    def _():
        o_ref[...] = y


def conv_layer(proj, row0, B, L, hist, conv_w, tr=256, tc=1024):
    tr = _tile(L, tr)
    nb = L // tr
    assert row0 % tr == 0
    hist_p = jnp.concatenate([jnp.zeros((B, SUBLANES - (C_CONV - 1), C_CONV_DIM), F32), hist], 1)
    return pl.pallas_call(
        functools.partial(_conv_kernel, tr=tr, tc=tc, nq=C_KEY_DIM // tc, nqk=2 * C_KEY_DIM // tc),
        grid=(B, nb, C_CONV_DIM // tc),
        in_specs=[pl.BlockSpec((tr, tc), lambda b, i, c: (row0 // tr + b * nb + i, c)),
                  pl.BlockSpec((SUBLANES, tc),
                               lambda b, i, c: (jnp.maximum((row0 + b * L + i * tr) // SUBLANES - 1, 0), c)),
                  pl.BlockSpec((None, SUBLANES, tc), lambda b, i, c: (b, 0, c)),
                  pl.BlockSpec((C_CONV, tc), lambda b, i, c: (0, c))],
        out_specs=pl.BlockSpec((tr, tc), lambda b, i, c: (b * nb + i, c)),
        out_shape=jax.ShapeDtypeStruct((B * L, C_CONV_DIM), F32),
        scratch_shapes=[pltpu.VMEM((tr + SUBLANES, tc), F32)],
        compiler_params=_cparams(("parallel", "parallel", "parallel")),
        name="conv_layer",
    )(proj, proj, hist_p, conv_w)


def _gate_kernel(ba_ref, al_ref, dt_ref, o_ref):
    ba = ba_ref[...]
    lane = lax.broadcasted_iota(jnp.int32, ba.shape, 1)
    x = ba + dt_ref[...]
    softplus = jnp.maximum(x, 0.0) + jnp.log(1.0 + jnp.exp(-jnp.abs(x)))
    g = jnp.where(jnp.logical_and(lane >= C_V_HEADS, lane < 2 * C_V_HEADS),
                  -jnp.exp(al_ref[...]) * softplus, 0.0)
    r = lax.broadcasted_iota(jnp.int32, (CHUNK, CHUNK), 0)
    c = lax.broadcasted_iota(jnp.int32, (CHUNK, CHUNK), 1)
    tril = jnp.where(r >= c, 1.0, 0.0).astype(F32)
    gc = _dot3(tril, g)
    o_ref[...] = jnp.where(lane < C_V_HEADS, _sigmoid(ba), gc)


def gate_layer(ba, a_log, dt_bias):
    T = ba.shape[0]
    pad = lambda v: jnp.zeros((1, LANES), F32).at[0, C_V_HEADS:2 * C_V_HEADS].set(v.astype(F32))
    vec = pl.BlockSpec((1, LANES), lambda i: (0, 0))
    blk = pl.BlockSpec((CHUNK, LANES), lambda i: (i, 0))
    return pl.pallas_call(
        _gate_kernel, grid=(T // CHUNK,), in_specs=[blk, vec, vec], out_specs=blk,
        out_shape=jax.ShapeDtypeStruct((T, LANES), F32),
        compiler_params=_cparams(("parallel",)), name="gate_layer",
    )(ba, pad(a_log), pad(dt_bias))


def _delta_kernel(q_ref, k_ref, v_ref, z_ref, col_ref, row_ref, s0_ref, ng_ref, o_ref, sout_ref,
                  S_ref, *, nc):
    c = pl.program_id(2)

    @pl.when(c == 0)
    def _():
        S_ref[...] = s0_ref[...]

    col = col_ref[...]
    rowg = row_ref[...]
    r = lax.broadcasted_iota(jnp.int32, (CHUNK, CHUNK), 0)
    cc = lax.broadcasted_iota(jnp.int32, (CHUNK, CHUNK), 1)
    incl, strict = r >= cc, r > cc
    eye = jnp.where(r == cc, 1.0, 0.0).astype(F32)
    rep = C_V_HEADS // C_QK_HEADS
    for hh in range(HEAD_GROUP):
        qs = slice((hh // rep) * C_HEAD_DIM, (hh // rep + 1) * C_HEAD_DIM)
        vs = slice(hh * C_HEAD_DIM, (hh + 1) * C_HEAD_DIM)
        qh, kh, vh = q_ref[:, qs], k_ref[:, qs], v_ref[:, vs]
        gcc = col[:, hh:hh + 1]
        beta = col[:, HEAD_GROUP + hh:HEAD_GROUP + hh + 1]
        gcr = rowg[hh:hh + 1, :]
        decay = jnp.where(incl, jnp.exp(jnp.where(incl, gcc - gcr, 0.0)), 0.0)
        khb = kh.astype(BF16)
        kb = kh * beta
        a_low = jnp.where(strict, _dot_nt(kb.astype(BF16), khb) * decay, 0.0)
        npow = -a_low
        t_inv = eye + npow
        for _ in range(int(math.log2(CHUNK)) - 1):
            npow = _dot3(npow, npow)
            t_inv = t_inv + _dot3(t_inv, npow)
        tb = t_inv.astype(BF16)
        egc = jnp.exp(gcc)
        u = _dot(tb, (vh * beta).astype(BF16))
        w = _dot(tb, (kb * egc).astype(BF16))
        a_qk = jnp.where(incl, _dot_nt(qh.astype(BF16), khb) * decay, 0.0)
        S = S_ref[hh]
        Sb = S.astype(BF16)
        v_new = u - _dot(w.astype(BF16), Sb)
        vnb = v_new.astype(BF16)
        o = _dot((qh * egc).astype(BF16), Sb) + _dot(a_qk.astype(BF16), vnb)
        gl = gcc[CHUNK - 1:CHUNK, :]
        kg = kh * jnp.exp(gl - gcc)
        S_ref[hh] = S * jnp.exp(gl) + _dot_tn(kg.astype(BF16), vnb)
        zz = z_ref[:, vs]
        ms = jnp.mean(o * o, axis=1, keepdims=True)
        y = o * lax.rsqrt(ms + 1e-6) * ng_ref[...] * (zz * _sigmoid(zz))
        o_ref[:, vs] = y.astype(o_ref.dtype)

    @pl.when(c == nc - 1)
    def _():
        sout_ref[...] = S_ref[...]


def delta_layer(qkvc, proj, gates, row0, B, L, state, norm_g):
    nc = L // CHUNK
    ng = C_V_HEADS // HEAD_GROUP
    rows = B * L
    assert L % CHUNK == 0 and row0 % CHUNK == 0
    gt = lax.slice_in_dim(gates, row0, row0 + rows, axis=0)
    beta = gt[:, :C_V_HEADS].reshape(rows, ng, HEAD_GROUP)
    gc = gt[:, C_V_HEADS:2 * C_V_HEADS].reshape(rows, ng, HEAD_GROUP)
    col = jnp.concatenate([gc, beta, jnp.zeros((rows, ng, LANES - 2 * HEAD_GROUP), F32)], -1)
    col = col.transpose(1, 0, 2)
    rowg = gt[:, C_V_HEADS:2 * C_V_HEADS].reshape(B * nc, CHUNK, C_V_HEADS).transpose(0, 2, 1)
    qw = HEAD_GROUP * C_HEAD_DIM * C_QK_HEADS // C_V_HEADS
    vw = HEAD_GROUP * C_HEAD_DIM
    kb0 = C_KEY_DIM // qw
    vb0 = 2 * C_KEY_DIM // vw
    zb0 = C_CONV_DIM // vw
    st_spec = pl.BlockSpec((None, HEAD_GROUP, C_HEAD_DIM, C_HEAD_DIM), lambda b, h, c: (b, h, 0, 0))
    o, s_new = pl.pallas_call(
        functools.partial(_delta_kernel, nc=nc),
        grid=(B, ng, nc),
        in_specs=[pl.BlockSpec((CHUNK, qw), lambda b, h, c: (b * nc + c, h)),
                  pl.BlockSpec((CHUNK, qw), lambda b, h, c: (b * nc + c, kb0 + h)),
                  pl.BlockSpec((CHUNK, vw), lambda b, h, c: (b * nc + c, vb0 + h)),
                  pl.BlockSpec((CHUNK, vw), lambda b, h, c: (row0 // CHUNK + b * nc + c, zb0 + h)),
                  pl.BlockSpec((None, CHUNK, LANES), lambda b, h, c: (h, b * nc + c, 0)),
                  pl.BlockSpec((None, HEAD_GROUP, CHUNK), lambda b, h, c: (b * nc + c, h, 0)),
                  st_spec,
                  pl.BlockSpec((1, C_HEAD_DIM), lambda b, h, c: (0, 0))],
        out_specs=[pl.BlockSpec((CHUNK, vw), lambda b, h, c: (b * nc + c, h)), st_spec],
        out_shape=[jax.ShapeDtypeStruct((rows, C_VAL_DIM), BF16),
                   jax.ShapeDtypeStruct(state.shape, F32)],
        scratch_shapes=[pltpu.VMEM((HEAD_GROUP, C_HEAD_DIM, C_HEAD_DIM), F32)],
        compiler_params=_cparams(("parallel", "parallel", "arbitrary")),
        name="delta_layer",
    )(qkvc, qkvc, qkvc, proj, col, rowg, state.astype(F32), norm_g.reshape(1, C_HEAD_DIM))
    return o, s_new


def _router_kernel(x_ref, w_ref, b_ref, o_ref):
    o_ref[...] = _dot3(x_ref[...], w_ref[...]) + b_ref[...]


def router_logits(x, w_router, b_router, tm=512):
    T = x.shape[0]
    tm = _tile(T, tm)
    w = jnp.zeros((D_MODEL, LANES), F32).at[:, :N_EXPERTS].set(w_router)
    b = jnp.zeros((1, LANES), F32).at[0, :N_EXPERTS].set(b_router)
    out = pl.pallas_call(
        _router_kernel, grid=(T // tm,),
        in_specs=[pl.BlockSpec((tm, D_MODEL), lambda i: (i, 0)),
                  pl.BlockSpec((D_MODEL, LANES), lambda i: (0, 0)),
                  pl.BlockSpec((1, LANES), lambda i: (0, 0))],
        out_specs=pl.BlockSpec((tm, LANES), lambda i: (i, 0)),
        out_shape=jax.ShapeDtypeStruct((T, LANES), F32),
        compiler_params=_cparams(("parallel",)), name="router",
    )(x, w, b)
    return out[:, :N_EXPERTS]


def _gmm1_kernel(be_ref, first_ref, nact_ref, x_ref, wg_ref, wu_ref, bg_ref, bu_ref, o_ref,
                 wgb_ref, wub_ref):
    r = pl.program_id(1)
    active = r < nact_ref[0]

    @pl.when(jnp.logical_and(active, first_ref[r] == 1))
    def _():
        wgb_ref[...] = wg_ref[...].astype(BF16)
        wub_ref[...] = wu_ref[...].astype(BF16)

    @pl.when(active)
    def _():
        x = x_ref[...]
        gate = jnp.minimum(_dot(x, wgb_ref[...]) + bg_ref[...], SWIGLU_LIMIT)
        up = jnp.clip(_dot(x, wub_ref[...]) + bu_ref[...], -SWIGLU_LIMIT, SWIGLU_LIMIT)
        act = (up + 1.0) * gate * _sigmoid(SWIGLU_ALPHA * gate)
        o_ref[...] = act.astype(o_ref.dtype)

    @pl.when(jnp.logical_not(active))
    def _():
        o_ref[...] = jnp.zeros(o_ref.shape, o_ref.dtype)


def gmm_gate_up(xs, w_gu, b_gu, layer, blk_exp, first, nact, tn=512):
    R = xs.shape[0]
    nb = R // MOE_TILE
    nj = D_FF // tn
    rmap = lambda r, na: jnp.minimum(r, na[0] - 1)
    b3 = b_gu.reshape(DEPTH, N_EXPERTS, 1, 2 * D_FF)
    grid_spec = pltpu.PrefetchScalarGridSpec(
        num_scalar_prefetch=3,
        grid=(nj, nb),
        in_specs=[pl.BlockSpec((MOE_TILE, D_MODEL), lambda j, r, be, fi, na: (rmap(r, na), 0)),
                  pl.BlockSpec((None, None, D_MODEL, tn), lambda j, r, be, fi, na: (layer, be[r], 0, j)),
                  pl.BlockSpec((None, None, D_MODEL, tn), lambda j, r, be, fi, na: (layer, be[r], 0, nj + j)),
                  pl.BlockSpec((None, None, 1, tn), lambda j, r, be, fi, na: (layer, be[r], 0, j)),
                  pl.BlockSpec((None, None, 1, tn), lambda j, r, be, fi, na: (layer, be[r], 0, nj + j))],
        out_specs=pl.BlockSpec((MOE_TILE, tn), lambda j, r, be, fi, na: (r, j)),
        scratch_shapes=[pltpu.VMEM((D_MODEL, tn), BF16)] * 2)
    return pl.pallas_call(
        _gmm1_kernel, grid_spec=grid_spec,
        out_shape=jax.ShapeDtypeStruct((R, D_FF), BF16),
        compiler_params=_cparams(("arbitrary", "arbitrary")),
        name="gmm_gate_up",
    )(blk_exp, first, nact, xs, w_gu, w_gu, b3, b3)


def _gmm2_kernel(be_ref, first_ref, nact_ref, a_ref, w_ref, b_ref, o_ref, wb_ref):
    r = pl.program_id(1)
    active = r < nact_ref[0]

    @pl.when(jnp.logical_and(active, first_ref[r] == 1))
    def _():
        wb_ref[...] = w_ref[...].astype(BF16)

    @pl.when(active)
    def _():
        o_ref[...] = _dot(a_ref[...], wb_ref[...]) + b_ref[...]

    @pl.when(jnp.logical_not(active))
    def _():
        o_ref[...] = jnp.zeros(o_ref.shape, o_ref.dtype)


def gmm_down(act, w_down, b_down, layer, blk_exp, first, nact, tn=1024):
    R = act.shape[0]
    nb = R // MOE_TILE
    rmap = lambda r, na: jnp.minimum(r, na[0] - 1)
    b3 = b_down.reshape(DEPTH, N_EXPERTS, 1, D_MODEL)
    grid_spec = pltpu.PrefetchScalarGridSpec(
        num_scalar_prefetch=3,
        grid=(D_MODEL // tn, nb),
        in_specs=[pl.BlockSpec((MOE_TILE, D_FF), lambda j, r, be, fi, na: (rmap(r, na), 0)),
                  pl.BlockSpec((None, None, D_FF, tn), lambda j, r, be, fi, na: (layer, be[r], 0, j)),
                  pl.BlockSpec((None, None, 1, tn), lambda j, r, be, fi, na: (layer, be[r], 0, j))],
        out_specs=pl.BlockSpec((MOE_TILE, tn), lambda j, r, be, fi, na: (r, j)),
        scratch_shapes=[pltpu.VMEM((D_FF, tn), BF16)])
    return pl.pallas_call(
        _gmm2_kernel, grid_spec=grid_spec,
        out_shape=jax.ShapeDtypeStruct((R, D_MODEL), F32),
        compiler_params=_cparams(("arbitrary", "arbitrary")),
        name="gmm_down",
    )(blk_exp, first, nact, act, w_down, b3)


def _combine_ln_kernel(ys_ref, gt_ref, x_ref, g_ref, b_ref, of_ref, ob_ref):
    f = ys_ref[0] * gt_ref[:, 0:1]
    for k in range(1, TOP_K):
        f = f + ys_ref[k] * gt_ref[:, k:k + 1]
    y = _layer_norm(DN_ALPHA * x_ref[...] + f, g_ref[...], b_ref[...])
    of_ref[...] = y
    ob_ref[...] = y.astype(BF16)


def combine_ln(ys_tok, gates, x, g, b, tm=256):
    T = x.shape[0]
    tm = _tile(T, tm)
    vec = pl.BlockSpec((1, D_MODEL), lambda i: (0, 0))
    row = pl.BlockSpec((tm, D_MODEL), lambda i: (i, 0))
    return pl.pallas_call(
        _combine_ln_kernel, grid=(T // tm,),
        in_specs=[pl.BlockSpec((TOP_K, tm, D_MODEL), lambda i: (0, i, 0)),
                  pl.BlockSpec((tm, TOP_K), lambda i: (i, 0)),
                  row, vec, vec],
        out_specs=[row, row],
        out_shape=[jax.ShapeDtypeStruct((T, D_MODEL), F32), jax.ShapeDtypeStruct((T, D_MODEL), BF16)],
        compiler_params=_cparams(("parallel",)), name="combine_ln",
    )(ys_tok, gates, x, g.reshape(1, -1), b.reshape(1, -1))


def moe_layer(x, xb, layer, w_router, b_router, w_gu, b_gu, w_down, b_down, g, b):
    T = x.shape[0]
    logits = router_logits(x, w_router, b_router)
    top_v, top_i = lax.top_k(logits, TOP_K)
    gates = jax.nn.softmax(top_v, axis=-1)
    n_as = T * TOP_K
    e_flat = top_i.reshape(n_as)
    tok_flat = jnp.arange(n_as, dtype=jnp.int32) // TOP_K
    order = jnp.argsort(e_flat)
    e_sorted = e_flat[order]
    counts = jnp.zeros((N_EXPERTS,), jnp.int32).at[e_flat].add(1)
    start = jnp.cumsum(counts) - counts
    padded = (counts + MOE_TILE - 1) // MOE_TILE * MOE_TILE
    pad_end = jnp.cumsum(padded)
    pad_start = pad_end - padded
    dest = pad_start[e_sorted] + jnp.arange(n_as, dtype=jnp.int32) - start[e_sorted]
    n_blocks = -(-n_as // MOE_TILE) + N_EXPERTS
    n_rows = n_blocks * MOE_TILE
    row_tok = jnp.zeros((n_rows,), jnp.int32).at[dest].set(tok_flat[order])
    pos = jnp.zeros((n_as,), jnp.int32).at[order].set(dest).reshape(T, TOP_K)
    blk_exp = jnp.minimum(jnp.searchsorted(pad_end, jnp.arange(n_blocks, dtype=jnp.int32) * MOE_TILE,
                                           side='right'), N_EXPERTS - 1).astype(jnp.int32)
    nact = (pad_end[-1] // MOE_TILE).astype(jnp.int32).reshape(1)
    first = jnp.concatenate([jnp.ones((1,), jnp.int32),
                             (blk_exp[1:] != blk_exp[:-1]).astype(jnp.int32)])
    xs = jnp.take(xb, row_tok, axis=0)
    act = gmm_gate_up(xs, w_gu, b_gu, layer, blk_exp, first, nact)
    ys = gmm_down(act, w_down, b_down, layer, blk_exp, first, nact)
    ys_tok = jnp.take(ys, pos.T, axis=0)
    return combine_ln(ys_tok, gates, x, g, b)


def kernel(x_prompt, x_sample, cache_a_k, cache_a_v, cache_pool, cache_conv, state_delta, cache_mem_k,
           cache_mem_v, mem_prompt, ln_g, ln_b, w_qkv_a, w_o_a, lam_q1, lam_k1, lam_q2, lam_k2, subln_g,
           w_pool, pool_scale, w_in_c, conv_w_c, a_log_c, dt_bias_c, norm_g_c, w_o_c, w_q_m, w_kv_m,
           w_o_m, w_router, b_router, w_gu, b_gu, w_down, b_down):
    Bp, Lp, _ = x_prompt.shape
    Bs, Ls, _ = x_sample.shape
    past = cache_a_k.shape[2]
    Tp, Ts = Bp * Lp, Bs * Ls
    M = mem_prompt.shape[1]

    x = jnp.concatenate([x_prompt.reshape(Tp, D_MODEL), x_sample.reshape(Ts, D_MODEL)], 0)
    xb = x.astype(BF16)
    pos_all = jnp.concatenate([jnp.tile(jnp.arange(Lp, dtype=jnp.int32), Bp),
                               jnp.tile(past + jnp.arange(Ls, dtype=jnp.int32), Bs)])
    tabs = rope_tables(pos_all)

    mem_b = mem_prompt.reshape(Bp * M, D_MODEL).astype(BF16)
    mem_kv = [matmul(mem_b, w_kv_m[i].astype(BF16), tm=Bp * M) for i in range(DEPTH)]
    ck = cache_mem_k.reshape(DEPTH, Bs, M, D_MODEL)
    cv = cache_mem_v.reshape(DEPTH, Bs, M, D_MODEL)
    cak = cache_a_k.reshape(cache_a_k.shape[0], Bs, past, D_MODEL)
    cav = cache_a_v.reshape(cache_a_v.shape[0], Bs, past, D_MODEL)

    new_k, new_v, new_pool, new_conv, new_delta = [], [], [], [], []
    for i in range(DEPTH):
        m, j = i % N_MIXERS, i // N_MIXERS
        g0, b0 = ln_g[i, 0], ln_b[i, 0]
        if m == 0:
            qkv = matmul_rope(xb, w_qkv_a[j].astype(BF16), tabs, 2 * D_MODEL)
            new_k.append(qkv[:, D_MODEL:2 * D_MODEL])
            new_v.append(qkv[:, 2 * D_MODEL:])
            lam_init = 0.8 - 0.6 * math.exp(-0.3 * i)
            lamv = jnp.stack([lam_q1[j], lam_k1[j], lam_q2[j], lam_k2[j]]).astype(F32)
            o_p = flash_prompt(qkv, Bp, Lp, lamv, subln_g[j], lam_init)
            o_s = flash_sample(qkv, Tp, Bs, Ls, cak, cav, j, lamv, subln_g[j], lam_init)
            x, xb = matmul_res_ln(jnp.concatenate([o_p, o_s], 0), w_o_a[j].astype(BF16), x, g0, b0)
        elif m == 1:
            new_pool.append(x)
            wpb = w_pool[j].astype(BF16)
            xp, xpb = pool_layer(x, 0, Bp, Lp, 0, jnp.zeros((Bp, POOL_HIST, D_MODEL), F32),
                                 wpb, pool_scale[j], g0, b0)
            xs_, xsb = pool_layer(x, Tp, Bs, Ls, past, cache_pool[j], wpb, pool_scale[j], g0, b0)
            x, xb = jnp.concatenate([xp, xs_], 0), jnp.concatenate([xpb, xsb], 0)
        else:
            n_main = C_CONV_DIM + C_VAL_DIM
            w_in = w_in_c[j]
            proj = matmul(xb, w_in[:, :n_main].astype(BF16))
            w_ba = jnp.zeros((D_MODEL, LANES), BF16).at[:, :2 * C_V_HEADS].set(w_in[:, n_main:].astype(BF16))
            ba = matmul(xb, w_ba, tn=LANES)
            new_conv.append(proj)
            gates = gate_layer(ba, a_log_c[j], dt_bias_c[j])
            outs = []
            for (row0, B, L, hist, st) in (
                    (0, Bp, Lp, jnp.zeros((Bp, C_CONV - 1, C_CONV_DIM), F32),
                     jnp.zeros((Bp, C_V_HEADS, C_HEAD_DIM, C_HEAD_DIM), F32)),
                    (Tp, Bs, Ls, cache_conv[j], state_delta[j])):
                qkvc = conv_layer(proj, row0, B, L, hist, conv_w_c[j])
                o, s_new = delta_layer(qkvc, proj, gates, row0, B, L, st, norm_g_c[j])
                outs.append(o)
                new_delta.append(s_new)
            x, xb = matmul_res_ln(jnp.concatenate(outs, 0), w_o_c[j].astype(BF16), x, g0, b0)

        wq = w_q_m[i].astype(BF16)
        kv = mem_kv[i]
        c_p = mem_attn(xb, wq, 0, Bp, Lp, kv, kv,
                       pl.BlockSpec((M, D_MODEL), lambda b, r: (b, 0)),
                       pl.BlockSpec((M, D_MODEL), lambda b, r: (b, 1)))
        cache_spec = pl.BlockSpec((None, None, M, D_MODEL), lambda b, r, i=i: (i, b, 0, 0))
        c_s = mem_attn(xb, wq, Tp, Bs, Ls, ck, cv, cache_spec, cache_spec)
        x, xb = matmul_res_ln(jnp.concatenate([c_p, c_s], 0), w_o_m[i].astype(BF16), x,
                              ln_g[i, 1], ln_b[i, 1])

        x, xb = moe_layer(x, xb, i, w_router[i], b_router[i], w_gu, b_gu, w_down, b_down,
                          ln_g[i, 2], ln_b[i, 2])

    def split(t, shape_p, shape_s):
        return t[:Tp].reshape(shape_p), t[Tp:].reshape(shape_s)

    y_prompt, y_sample = split(x, (Bp, Lp, D_MODEL), (Bs, Ls, D_MODEL))
    kp, ks = zip(*[split(t, (Bp, Lp, 2 * A_HEADS, A_HEAD_DIM), (Bs, Ls, 2 * A_HEADS, A_HEAD_DIM))
                   for t in new_k])
    vp, vs = zip(*[split(t, (Bp, Lp, A_HEADS, A_V_DIM), (Bs, Ls, A_HEADS, A_V_DIM)) for t in new_v])
    pp, ps = zip(*[split(t, (Bp, Lp, D_MODEL), (Bs, Ls, D_MODEL)) for t in new_pool])
    cp, cs = zip(*[split(t[:, :C_CONV_DIM], (Bp, Lp, C_CONV_DIM), (Bs, Ls, C_CONV_DIM)) for t in new_conv])
    mk = jnp.stack([kv[:, :D_MODEL].reshape(Bp, M, M_HEADS, M_HEAD_DIM) for kv in mem_kv])
    mv = jnp.stack([kv[:, D_MODEL:].reshape(Bp, M, M_HEADS, M_HEAD_DIM) for kv in mem_kv])
    return (y_prompt, y_sample, jnp.stack(kp), jnp.stack(vp), jnp.stack(ks), jnp.stack(vs),
            jnp.stack([t[:, -POOL_HIST:] for t in pp]), jnp.stack([t[:, -POOL_HIST:] for t in ps]),
            jnp.stack([t[:, -(C_CONV - 1):] for t in cp]), jnp.stack([t[:, -(C_CONV - 1):] for t in cs]),
            jnp.stack(new_delta[0::2]), jnp.stack(new_delta[1::2]), mk, mv)
```

```python
import functools
import math

import jax
import jax.numpy as jnp
from jax import lax
from jax.experimental import pallas as pl
from jax.experimental.pallas import tpu as pltpu

F32 = jnp.float32
BF16 = jnp.bfloat16

D_MODEL = 2048
DEPTH = 4
CHUNK = 64
N_MIXERS = 3
A_HEADS = 16
A_HEAD_DIM = D_MODEL // (2 * A_HEADS)
A_V_DIM = 2 * A_HEAD_DIM
ROT_DIM = A_HEAD_DIM // 4
ROPE_THETA = 500000.0
POOL_WINDOWS = (2, 4, 8, 16)
POOL_GROUP_DIM = D_MODEL // len(POOL_WINDOWS)
POOL_HIST = max(POOL_WINDOWS) - 1
C_QK_HEADS = 16
C_V_HEADS = 32
C_HEAD_DIM = 128
C_KEY_DIM = C_QK_HEADS * C_HEAD_DIM
C_VAL_DIM = C_V_HEADS * C_HEAD_DIM
C_CONV_DIM = 2 * C_KEY_DIM + C_VAL_DIM
C_CONV = 4
M_HEADS = 4
M_HEAD_DIM = D_MODEL // M_HEADS
N_EXPERTS = 32
TOP_K = 4
D_FF = D_MODEL
SWIGLU_LIMIT = 7.0
SWIGLU_ALPHA = 1.702
DN_ALPHA = (2 * DEPTH) ** 0.25
LN_EPS = 1e-5

LANES = 128
SUBLANES = 8
VMEM_LIMIT = 52 * 1024 * 1024
HEAD_GROUP = 16
MOE_TILE = 256


def _cparams(sem):
    return pltpu.CompilerParams(dimension_semantics=sem, vmem_limit_bytes=VMEM_LIMIT)


def _tile(n, pref, mult=SUBLANES):
    t = min(pref, n)
    while t > mult and (n % t or t % mult):
        t -= mult
    assert n % t == 0, (n, pref)
    return t


def _dot(a, b):
    return jnp.dot(a, b, preferred_element_type=F32)


def _dot_nt(a, b):
    return lax.dot_general(a, b, (((1,), (1,)), ((), ())), preferred_element_type=F32)


def _dot_tn(a, b):
    return lax.dot_general(a, b, (((0,), (0,)), ((), ())), preferred_element_type=F32)


def _split(a):
    hi = a.astype(BF16)
    lo = (a - hi.astype(F32)).astype(BF16)
    return hi, lo


def _dot3(a, b):
    ah, al = _split(a)
    bh, bl = _split(b)
    return _dot(ah, bh) + _dot(ah, bl) + _dot(al, bh)


def _sigmoid(x):
    return 1.0 / (1.0 + jnp.exp(-x))


def _layer_norm(y, g, b):
    mu = jnp.mean(y, axis=-1, keepdims=True)
    d = y - mu
    var = jnp.mean(d * d, axis=-1, keepdims=True)
    return d * lax.rsqrt(var + LN_EPS) * g + b


def _mm_kernel(a_ref, w_ref, o_ref):
    o_ref[...] = _dot(a_ref[...], w_ref[...]).astype(o_ref.dtype)


def matmul(a, w, out_dtype=F32, tm=512, tn=1024):
    M, K = a.shape
    N = w.shape[1]
    tm, tn = _tile(M, tm), _tile(N, tn, LANES)
    return pl.pallas_call(
        _mm_kernel,
        grid=(N // tn, M // tm),
        in_specs=[pl.BlockSpec((tm, K), lambda j, i: (i, 0)),
                  pl.BlockSpec((K, tn), lambda j, i: (0, j))],
        out_specs=pl.BlockSpec((tm, tn), lambda j, i: (i, j)),
        out_shape=jax.ShapeDtypeStruct((M, N), out_dtype),
        compiler_params=_cparams(("parallel", "parallel")),
        name="matmul",
    )(a, w)


def _mm_rope_kernel(a_ref, w_ref, c_ref, s1_ref, s2_ref, o_ref, *, n_rope, tn):
    j = pl.program_id(0)
    acc = _dot(a_ref[...], w_ref[...])

    @pl.when(j < n_rope)
    def _():
        c, s1, s2 = c_ref[...], s1_ref[...], s2_ref[...]
        half = ROT_DIM // 2
        for cb in range(tn // LANES):
            x = acc[:, cb * LANES:(cb + 1) * LANES]
            o_ref[:, cb * LANES:(cb + 1) * LANES] = (
                x * c + pltpu.roll(x, half, 1) * s1 + pltpu.roll(x, LANES - half, 1) * s2)

    @pl.when(j >= n_rope)
    def _():
        o_ref[...] = acc


def matmul_rope(a, w, tabs, n_rope_cols, tm=512, tn=1024):
    M, K = a.shape
    N = w.shape[1]
    tm, tn = _tile(M, tm), _tile(N, tn, LANES)
    tab_spec = pl.BlockSpec((tm, LANES), lambda j, i: (i, 0))
    return pl.pallas_call(
        functools.partial(_mm_rope_kernel, n_rope=n_rope_cols // tn, tn=tn),
        grid=(N // tn, M // tm),
        in_specs=[pl.BlockSpec((tm, K), lambda j, i: (i, 0)),
                  pl.BlockSpec((K, tn), lambda j, i: (0, j)),
                  tab_spec, tab_spec, tab_spec],
        out_specs=pl.BlockSpec((tm, tn), lambda j, i: (i, j)),
        out_shape=jax.ShapeDtypeStruct((M, N), F32),
        compiler_params=_cparams(("parallel", "parallel")),
        name="matmul_rope",
    )(a, w, *tabs)


def rope_tables(pos):
    inv = ROPE_THETA ** (-jnp.arange(0, ROT_DIM, 2, dtype=F32) / ROT_DIM)
    ang = pos.astype(F32)[:, None] * inv[None, :]
    cos, sin = jnp.cos(ang), jnp.sin(ang)
    half = ROT_DIM // 2
    ones = jnp.ones((pos.shape[0], A_HEAD_DIM - ROT_DIM), F32)
    zeros = jnp.zeros((pos.shape[0], A_HEAD_DIM - ROT_DIM), F32)
    zh = jnp.zeros((pos.shape[0], half), F32)
    c = jnp.concatenate([cos, cos, ones], 1)
    s1 = jnp.concatenate([zh, sin, zeros], 1)
    s2 = jnp.concatenate([-sin, zh, zeros], 1)
    rep = LANES // A_HEAD_DIM
    return tuple(jnp.tile(t, (1, rep)) for t in (c, s1, s2))


def _mm_res_ln_kernel(a_ref, w_ref, x_ref, g_ref, b_ref, of_ref, ob_ref, acc_ref, *, nk):
    k = pl.program_id(1)

    def finish(h):
        y = _layer_norm(DN_ALPHA * x_ref[...] + h, g_ref[...], b_ref[...])
        of_ref[...] = y
        ob_ref[...] = y.astype(BF16)

    part = _dot(a_ref[...], w_ref[...])
    if nk == 1:
        finish(part)
    else:
        @pl.when(k == 0)
        def _():
            acc_ref[...] = part

        @pl.when(jnp.logical_and(k > 0, k < nk - 1))
        def _():
            acc_ref[...] += part

        @pl.when(k == nk - 1)
        def _():
            finish(acc_ref[...] + part)


def matmul_res_ln(a, w, x, g, b, tm=256, tk=2048):
    M, K = a.shape
    N = w.shape[1]
    tm, tk = _tile(M, tm), _tile(K, tk, LANES)
    nk = K // tk
    vec = pl.BlockSpec((1, N), lambda i, k: (0, 0))
    row = pl.BlockSpec((tm, N), lambda i, k: (i, 0))
    return pl.pallas_call(
        functools.partial(_mm_res_ln_kernel, nk=nk),
        grid=(M // tm, nk),
        in_specs=[pl.BlockSpec((tm, tk), lambda i, k: (i, k)),
                  pl.BlockSpec((tk, N), lambda i, k: (k, 0)),
                  row, vec, vec],
        out_specs=[row, row],
        out_shape=[jax.ShapeDtypeStruct((M, N), F32), jax.ShapeDtypeStruct((M, N), BF16)],
        scratch_shapes=[pltpu.VMEM((tm, N), F32)],
        compiler_params=_cparams(("parallel", "arbitrary")),
        name="matmul_res_ln",
    )(a, w, x, g.reshape(1, N), b.reshape(1, N))


def _attn_init(m_ref, l_ref, acc_ref):
    m_ref[...] = jnp.full(m_ref.shape, -jnp.inf, F32)
    l_ref[...] = jnp.zeros(l_ref.shape, F32)
    acc_ref[...] = jnp.zeros(acc_ref.shape, F32)


def _attn_step(q_ref, k_ref, v_ref, m_ref, l_ref, acc_ref, n_heads, mask):
    tq = q_ref.shape[0]
    lane = lax.broadcasted_iota(jnp.int32, (tq, LANES), 1)
    first = lane < A_HEAD_DIM
    hsl = lambda h: slice(h * LANES, (h + 1) * LANES)
    maps = [(h, mp) for h in range(n_heads) for mp in range(2)]
    q = [q_ref[:, hsl(h)] * (A_HEAD_DIM ** -0.5) for h in range(n_heads)]
    k = [k_ref[:, hsl(h)].astype(BF16) for h in range(n_heads)]
    v = [v_ref[:, hsl(h)].astype(BF16) for h in range(n_heads)]
    s = [_dot_nt(jnp.where(first if mp == 0 else jnp.logical_not(first), q[h], 0.0).astype(BF16), k[h])
         for h, mp in maps]
    if mask is not None:
        s = [jnp.where(mask, x, -jnp.inf) for x in s]
    m_prev = [m_ref[i] for i in range(len(maps))]
    m_new = [jnp.maximum(m_prev[i], jnp.max(s[i], axis=1, keepdims=True)) for i in range(len(maps))]
    alpha = [jnp.exp(m_prev[i] - m_new[i]) for i in range(len(maps))]
    p = [jnp.exp(s[i] - m_new[i][:, :1]) for i in range(len(maps))]
    for i in range(len(maps)):
        l_ref[i] = alpha[i] * l_ref[i] + jnp.sum(p[i], axis=1, keepdims=True)
        m_ref[i] = m_new[i]
    pv = [_dot(p[i].astype(BF16), v[h]) for i, (h, mp) in enumerate(maps)]
    for i in range(len(maps)):
        acc_ref[i] = alpha[i] * acc_ref[i] + pv[i]


def _attn_finish(lamv_ref, g_ref, o_ref, l_ref, acc_ref, n_heads, lam_init):
    lamv = lamv_ref[...]
    lam = (jnp.exp(jnp.sum(lamv[0:1] * lamv[1:2], axis=1, keepdims=True))
           - jnp.exp(jnp.sum(lamv[2:3] * lamv[3:4], axis=1, keepdims=True)) + lam_init)
    for h in range(n_heads):
        o = acc_ref[2 * h] / l_ref[2 * h] - lam * (acc_ref[2 * h + 1] / l_ref[2 * h + 1])
        ms = jnp.mean(o * o, axis=1, keepdims=True)
        y = o * lax.rsqrt(ms + 1e-5) * g_ref[...] * (1.0 - lam_init)
        o_ref[:, h * LANES:(h + 1) * LANES] = y.astype(o_ref.dtype)


def _flash_prompt_kernel(qt_ref, kt_ref, q_ref, k_ref, v_ref, lamv_ref, g_ref, o_ref,
                         m_ref, l_ref, acc_ref, *, hpb, lam_init):
    p = pl.program_id(2)
    qb, kb = qt_ref[p], kt_ref[p]
    tq = q_ref.shape[0]

    @pl.when(kb == 0)
    def _():
        _attn_init(m_ref, l_ref, acc_ref)

    @pl.when(kb < qb)
    def _():
        _attn_step(q_ref, k_ref, v_ref, m_ref, l_ref, acc_ref, hpb, None)

    @pl.when(kb == qb)
    def _():
        r = lax.broadcasted_iota(jnp.int32, (tq, tq), 0) // CHUNK
        c = lax.broadcasted_iota(jnp.int32, (tq, tq), 1) // CHUNK
        _attn_step(q_ref, k_ref, v_ref, m_ref, l_ref, acc_ref, hpb, r >= c)
        _attn_finish(lamv_ref, g_ref, o_ref, l_ref, acc_ref, hpb, lam_init)


def flash_prompt(qkv, B, L, lamv, g, lam_init, tq=1024, hpb=1):
    tq = _tile(L, tq, CHUNK)
    nq = L // tq
    pairs = [(qb, kb) for qb in range(nq) for kb in range(qb + 1)]
    qt = jnp.asarray([p[0] for p in pairs], jnp.int32)
    kt = jnp.asarray([p[1] for p in pairs], jnp.int32)
    w = hpb * LANES
    hb = D_MODEL // w
    grid_spec = pltpu.PrefetchScalarGridSpec(
        num_scalar_prefetch=2,
        grid=(B, A_HEADS // hpb, len(pairs)),
        in_specs=[pl.BlockSpec((tq, w), lambda b, h, p, qt, kt: (b * nq + qt[p], h)),
                  pl.BlockSpec((tq, w), lambda b, h, p, qt, kt: (b * nq + kt[p], hb + h)),
                  pl.BlockSpec((tq, w), lambda b, h, p, qt, kt: (b * nq + kt[p], 2 * hb + h)),
                  pl.BlockSpec((4, A_HEAD_DIM), lambda b, h, p, qt, kt: (0, 0)),
                  pl.BlockSpec((1, LANES), lambda b, h, p, qt, kt: (0, 0))],
        out_specs=pl.BlockSpec((tq, w), lambda b, h, p, qt, kt: (b * nq + qt[p], h)),
        scratch_shapes=[pltpu.VMEM((2 * hpb, tq, LANES), F32)] * 3)
    return pl.pallas_call(
        functools.partial(_flash_prompt_kernel, hpb=hpb, lam_init=lam_init),
        grid_spec=grid_spec,
        out_shape=jax.ShapeDtypeStruct((B * L, D_MODEL), BF16),
        compiler_params=_cparams(("parallel", "parallel", "arbitrary")),
        name="flash_prompt",
    )(qt, kt, qkv, qkv, qkv, lamv, g.reshape(1, LANES))


def _attn_sample_kernel(q_ref, kh_ref, vh_ref, kn_ref, vn_ref, lamv_ref, g_ref, o_ref, *, hpb, lam_init):
    tq = q_ref.shape[0]
    lane = lax.broadcasted_iota(jnp.int32, (tq, LANES), 1)
    first = lane < A_HEAD_DIM
    hsl = lambda h: slice(h * LANES, (h + 1) * LANES)
    maps = [(h, mp) for h in range(hpb) for mp in range(2)]
    lamv = lamv_ref[...]
    lam = (jnp.exp(jnp.sum(lamv[0:1] * lamv[1:2], axis=1, keepdims=True))
           - jnp.exp(jnp.sum(lamv[2:3] * lamv[3:4], axis=1, keepdims=True)) + lam_init)
    q = [q_ref[:, hsl(h)] * (A_HEAD_DIM ** -0.5) for h in range(hpb)]
    qm = [jnp.where(first if mp == 0 else jnp.logical_not(first), q[h], 0.0).astype(BF16) for h, mp in maps]
    sh = [_dot_nt(qm[i], kh_ref[:, hsl(h)].astype(BF16)) for i, (h, mp) in enumerate(maps)]
    sn = [_dot_nt(qm[i], kn_ref[:, hsl(h)].astype(BF16)) for i, (h, mp) in enumerate(maps)]
    m = [jnp.maximum(jnp.max(sh[i], axis=1, keepdims=True), jnp.max(sn[i], axis=1, keepdims=True))
         for i in range(len(maps))]
    ph = [jnp.exp(sh[i] - m[i]) for i in range(len(maps))]
    pn = [jnp.exp(sn[i] - m[i]) for i in range(len(maps))]
    l = [jnp.sum(ph[i], axis=1, keepdims=True) + jnp.sum(pn[i], axis=1, keepdims=True) for i in range(len(maps))]
    for h in range(hpb):
        wh = ph[2 * h] / l[2 * h] - lam * (ph[2 * h + 1] / l[2 * h + 1])
        wn = pn[2 * h] / l[2 * h] - lam * (pn[2 * h + 1] / l[2 * h + 1])
        o = (_dot(wh.astype(BF16), vh_ref[:, hsl(h)].astype(BF16))
             + _dot(wn.astype(BF16), vn_ref[:, hsl(h)].astype(BF16)))
        ms = jnp.mean(o * o, axis=1, keepdims=True)
        y = o * lax.rsqrt(ms + 1e-5) * g_ref[...] * (1.0 - lam_init)
        o_ref[:, hsl(h)] = y.astype(o_ref.dtype)


def attn_sample(qkv, row0, B, L, k_cache, v_cache, j, lamv, g, lam_init, hpb=4):
    past = k_cache.shape[2]
    assert L == CHUNK and past % CHUNK == 0 and row0 % L == 0
    rb = row0 // L
    w = hpb * LANES
    hb = D_MODEL // w
    new = lambda c: pl.BlockSpec((L, w), lambda b, h: (rb + b, c * hb + h))
    hist = pl.BlockSpec((None, None, past, w), lambda b, h: (j, b, 0, h))
    return pl.pallas_call(
        functools.partial(_attn_sample_kernel, hpb=hpb, lam_init=lam_init),
        grid=(B, hb),
        in_specs=[new(0), hist, hist, new(1), new(2),
                  pl.BlockSpec((4, A_HEAD_DIM), lambda b, h: (0, 0)),
                  pl.BlockSpec((1, LANES), lambda b, h: (0, 0))],
        out_specs=pl.BlockSpec((L, w), lambda b, h: (b, h)),
        out_shape=jax.ShapeDtypeStruct((B * L, D_MODEL), BF16),
        compiler_params=_cparams(("parallel", "parallel")),
        name="attn_sample",
    )(qkv, k_cache, v_cache, qkv, qkv, lamv, g.reshape(1, LANES))


def _mem_attn_kernel(x_ref, wq_ref, mk_ref, mv_ref, o_ref):
    q = _dot(x_ref[...], wq_ref[...])
    for h in range(M_HEADS):
        sl = slice(h * M_HEAD_DIM, (h + 1) * M_HEAD_DIM)
        s = _dot_nt(q[:, sl].astype(BF16), mk_ref[:, sl].astype(BF16)) * (M_HEAD_DIM ** -0.5)
        p = jnp.exp(s - jnp.max(s, axis=1, keepdims=True))
        p = p / jnp.sum(p, axis=1, keepdims=True)
        o_ref[:, sl] = _dot(p.astype(BF16), mv_ref[:, sl].astype(BF16)).astype(o_ref.dtype)


def mem_attn(xb, wq, row0, B, L, k_arr, v_arr, k_spec, v_spec, tm=512):
    tm = _tile(L, tm)
    nb = L // tm
    rb = row0 // tm
    assert row0 % tm == 0
    return pl.pallas_call(
        _mem_attn_kernel,
        grid=(B, nb),
        in_specs=[pl.BlockSpec((tm, D_MODEL), lambda b, i: (rb + b * nb + i, 0)),
                  pl.BlockSpec((D_MODEL, D_MODEL), lambda b, i: (0, 0)),
                  k_spec, v_spec],
        out_specs=pl.BlockSpec((tm, D_MODEL), lambda b, i: (b * nb + i, 0)),
        out_shape=jax.ShapeDtypeStruct((B * L, D_MODEL), BF16),
        compiler_params=_cparams(("parallel", "parallel")),
        name="mem_attn",
    )(xb, wq, k_arr, v_arr)


def _pool_kernel(x_ref, prev_ref, hist_ref, w_ref, sc_ref, g_ref, b_ref, of_ref, ob_ref,
                 cat_ref, y_ref, *, tr, pos0):
    i = pl.program_id(1)
    H = POOL_HIST + 1
    cat_ref[0:H, :] = jnp.where(i == 0, hist_ref[...], prev_ref[...])
    cat_ref[H:, :] = x_ref[...]
    pos = pos0 + i * tr + lax.broadcasted_iota(jnp.int32, (tr, 1), 0)
    for gi, w in enumerate(POOL_WINDOWS):
        cs = slice(gi * POOL_GROUP_DIM, (gi + 1) * POOL_GROUP_DIM)
        cur = x_ref[:, cs]
        win = cur
        for jj in range(1, w):
            win = win + cat_ref[H - jj:H - jj + tr, cs]
        cnt = jnp.minimum(pos + 1, w).astype(F32)
        pooled = win / cnt - cur
        y_ref[:, cs] = _dot(pooled.astype(BF16), w_ref[gi]) * sc_ref[:, cs]
    y = _layer_norm(DN_ALPHA * x_ref[...] + y_ref[...], g_ref[...], b_ref[...])
    of_ref[...] = y
    ob_ref[...] = y.astype(BF16)


def pool_layer(x, row0, B, L, pos0, hist, w_pool_b, pool_scale, g, b, tr=256):
    tr = _tile(L, tr, POOL_HIST + 1)
    H = POOL_HIST + 1
    nb = L // tr
    assert row0 % tr == 0
    hist_p = jnp.concatenate([jnp.zeros((B, 1, D_MODEL), F32), hist], 1)
    vec = pl.BlockSpec((1, D_MODEL), lambda b_, i: (0, 0))
    row = pl.BlockSpec((tr, D_MODEL), lambda b_, i: (b_ * nb + i, 0))
    return pl.pallas_call(
        functools.partial(_pool_kernel, tr=tr, pos0=pos0),
        grid=(B, nb),
        in_specs=[pl.BlockSpec((tr, D_MODEL), lambda b_, i: (row0 // tr + b_ * nb + i, 0)),
                  pl.BlockSpec((H, D_MODEL),
                               lambda b_, i: (jnp.maximum((row0 + b_ * L + i * tr) // H - 1, 0), 0)),
                  pl.BlockSpec((None, H, D_MODEL), lambda b_, i: (b_, 0, 0)),
                  pl.BlockSpec((len(POOL_WINDOWS), POOL_GROUP_DIM, POOL_GROUP_DIM), lambda b_, i: (0, 0, 0)),
                  vec, vec, vec],
        out_specs=[row, row],
        out_shape=[jax.ShapeDtypeStruct((B * L, D_MODEL), F32),
                   jax.ShapeDtypeStruct((B * L, D_MODEL), BF16)],
        scratch_shapes=[pltpu.VMEM((tr + H, D_MODEL), F32), pltpu.VMEM((tr, D_MODEL), F32)],
        compiler_params=_cparams(("parallel", "parallel")),
        name="pool_layer",
    )(x, x, hist_p, w_pool_b, pool_scale.reshape(1, -1), g.reshape(1, -1), b.reshape(1, -1))


def _conv_kernel(cur_ref, prev_ref, hist_ref, w_ref, o_ref, cat_ref, *, tr, tc, nq, nqk):
    i = pl.program_id(1)
    c = pl.program_id(2)
    cat_ref[0:SUBLANES, :] = jnp.where(i == 0, hist_ref[...], prev_ref[...])
    cat_ref[SUBLANES:, :] = cur_ref[...]
    acc = cur_ref[...] * w_ref[C_CONV - 1:C_CONV, :]
    for jj in range(C_CONV - 1):
        s = SUBLANES - (C_CONV - 1) + jj
        acc = acc + cat_ref[s:s + tr, :] * w_ref[jj:jj + 1, :]
    y = acc * _sigmoid(acc)

    @pl.when(c < nqk)
    def _():
        scale = jnp.where(c < nq, C_HEAD_DIM ** -0.5, 1.0)
        for h in range(tc // C_HEAD_DIM):
            sl = slice(h * C_HEAD_DIM, (h + 1) * C_HEAD_DIM)
            yh = y[:, sl]
            o_ref[:, sl] = yh * (lax.rsqrt(jnp.sum(yh * yh, axis=1, keepdims=True) + 1e-6) * scale)

    @pl.when(c >= nqk)
    def _():
        o_ref[...] = y


def conv_layer(proj, row0, B, L, hist, conv_w, tr=256, tc=1024):
    tr = _tile(L, tr)
    nb = L // tr
    assert row0 % tr == 0
    hist_p = jnp.concatenate([jnp.zeros((B, SUBLANES - (C_CONV - 1), C_CONV_DIM), F32), hist], 1)
    return pl.pallas_call(
        functools.partial(_conv_kernel, tr=tr, tc=tc, nq=C_KEY_DIM // tc, nqk=2 * C_KEY_DIM // tc),
        grid=(B, nb, C_CONV_DIM // tc),
        in_specs=[pl.BlockSpec((tr, tc), lambda b, i, c: (row0 // tr + b * nb + i, c)),
                  pl.BlockSpec((SUBLANES, tc),
                               lambda b, i, c: (jnp.maximum((row0 + b * L + i * tr) // SUBLANES - 1, 0), c)),
                  pl.BlockSpec((None, SUBLANES, tc), lambda b, i, c: (b, 0, c)),
                  pl.BlockSpec((C_CONV, tc), lambda b, i, c: (0, c))],
        out_specs=pl.BlockSpec((tr, tc), lambda b, i, c: (b * nb + i, c)),
        out_shape=jax.ShapeDtypeStruct((B * L, C_CONV_DIM), F32),
        scratch_shapes=[pltpu.VMEM((tr + SUBLANES, tc), F32)],
        compiler_params=_cparams(("parallel", "parallel", "parallel")),
        name="conv_layer",
    )(proj, proj, hist_p, conv_w)


def _gate_kernel(ba_ref, al_ref, dt_ref, o_ref):
    ba = ba_ref[...]
    lane = lax.broadcasted_iota(jnp.int32, ba.shape, 1)
    x = ba + dt_ref[...]
    softplus = jnp.maximum(x, 0.0) + jnp.log(1.0 + jnp.exp(-jnp.abs(x)))
    g = jnp.where(jnp.logical_and(lane >= C_V_HEADS, lane < 2 * C_V_HEADS),
                  -jnp.exp(al_ref[...]) * softplus, 0.0)
    r = lax.broadcasted_iota(jnp.int32, (CHUNK, CHUNK), 0)
    c = lax.broadcasted_iota(jnp.int32, (CHUNK, CHUNK), 1)
    tril = jnp.where(r >= c, 1.0, 0.0).astype(F32)
    gc = _dot3(tril, g)
    o_ref[...] = jnp.where(lane < C_V_HEADS, _sigmoid(ba), gc)


def gate_layer(ba, a_log, dt_bias):
    T = ba.shape[0]
    pad = lambda v: jnp.zeros((1, LANES), F32).at[0, C_V_HEADS:2 * C_V_HEADS].set(v.astype(F32))
    vec = pl.BlockSpec((1, LANES), lambda i: (0, 0))
    blk = pl.BlockSpec((CHUNK, LANES), lambda i: (i, 0))
    return pl.pallas_call(
        _gate_kernel, grid=(T // CHUNK,), in_specs=[blk, vec, vec], out_specs=blk,
        out_shape=jax.ShapeDtypeStruct((T, LANES), F32),
        compiler_params=_cparams(("parallel",)), name="gate_layer",
    )(ba, pad(a_log), pad(dt_bias))


def _delta_kernel(q_ref, k_ref, v_ref, z_ref, col_ref, row_ref, s0_ref, ng_ref, o_ref, sout_ref,
                  S_ref, *, nc):
    c = pl.program_id(2)

    @pl.when(c == 0)
    def _():
        S_ref[...] = s0_ref[...]

    col = col_ref[...]
    rowg = row_ref[...]
    r = lax.broadcasted_iota(jnp.int32, (CHUNK, CHUNK), 0)
    cc = lax.broadcasted_iota(jnp.int32, (CHUNK, CHUNK), 1)
    incl, strict = r >= cc, r > cc
    eye = jnp.where(r == cc, 1.0, 0.0).astype(F32)
    rep = C_V_HEADS // C_QK_HEADS
    heads = range(HEAD_GROUP)
    hsl = lambda h: slice(h * C_HEAD_DIM, (h + 1) * C_HEAD_DIM)
    kh = [k_ref[:, hsl(h)] for h in range(HEAD_GROUP // rep)]
    khb = [k.astype(BF16) for k in kh]
    gcc = [col[:, h:h + 1] for h in heads]
    beta = [col[:, HEAD_GROUP + h:HEAD_GROUP + h + 1] for h in heads]
    decay = [jnp.where(incl, jnp.exp(jnp.where(incl, gcc[h] - rowg[h:h + 1, :], 0.0)), 0.0) for h in heads]
    kb = [kh[h // rep] * beta[h] for h in heads]
    kk = [_dot_nt(kb[h].astype(BF16), khb[h // rep]) for h in heads]
    qk = [_dot_nt(q_ref[:, hsl(h // rep)].astype(BF16), khb[h // rep]) for h in heads]
    npow = [jnp.where(strict, -kk[h] * decay[h], 0.0) for h in heads]
    t_inv = [eye + npow[h] for h in heads]
    for _ in range(int(math.log2(CHUNK)) - 1):
        npow = [_dot3(n, n) for n in npow]
        t_inv = [t_inv[h] + _dot3(t_inv[h], npow[h]) for h in heads]
    tb = [t.astype(BF16) for t in t_inv]
    egc = [jnp.exp(g) for g in gcc]
    u = [_dot(tb[h], (v_ref[:, hsl(h)] * beta[h]).astype(BF16)) for h in heads]
    w = [_dot(tb[h], (kb[h] * egc[h]).astype(BF16)) for h in heads]
    S = [S_ref[h] for h in heads]
    Sb = [s.astype(BF16) for s in S]
    vnb = [(u[h] - _dot(w[h].astype(BF16), Sb[h])).astype(BF16) for h in heads]
    a_qk = [jnp.where(incl, qk[h] * decay[h], 0.0).astype(BF16) for h in heads]
    o = [_dot((q_ref[:, hsl(h // rep)] * egc[h]).astype(BF16), Sb[h]) + _dot(a_qk[h], vnb[h]) for h in heads]
    for h in heads:
        gl = gcc[h][CHUNK - 1:CHUNK, :]
        kg = kh[h // rep] * jnp.exp(gl - gcc[h])
        S_ref[h] = S[h] * jnp.exp(gl) + _dot_tn(kg.astype(BF16), vnb[h])
    for h in heads:
        zz = z_ref[:, hsl(h)]
        ms = jnp.mean(o[h] * o[h], axis=1, keepdims=True)
        y = o[h] * lax.rsqrt(ms + 1e-6) * ng_ref[...] * (zz * _sigmoid(zz))
        o_ref[:, hsl(h)] = y.astype(o_ref.dtype)

    @pl.when(c == nc - 1)
    def _():
        sout_ref[...] = S_ref[...]


def delta_layer(qkvc, proj, gates, row0, B, L, state, norm_g):
    nc = L // CHUNK
    ng = C_V_HEADS // HEAD_GROUP
    rows = B * L
    assert L % CHUNK == 0 and row0 % CHUNK == 0
    gt = lax.slice_in_dim(gates, row0, row0 + rows, axis=0)
    beta = gt[:, :C_V_HEADS].reshape(rows, ng, HEAD_GROUP)
    gc = gt[:, C_V_HEADS:2 * C_V_HEADS].reshape(rows, ng, HEAD_GROUP)
    col = jnp.concatenate([gc, beta, jnp.zeros((rows, ng, LANES - 2 * HEAD_GROUP), F32)], -1)
    col = col.transpose(1, 0, 2)
    rowg = gt[:, C_V_HEADS:2 * C_V_HEADS].reshape(B * nc, CHUNK, C_V_HEADS).transpose(0, 2, 1)
    qw = HEAD_GROUP * C_HEAD_DIM * C_QK_HEADS // C_V_HEADS
    vw = HEAD_GROUP * C_HEAD_DIM
    kb0 = C_KEY_DIM // qw
    vb0 = 2 * C_KEY_DIM // vw
    zb0 = C_CONV_DIM // vw
    st_spec = pl.BlockSpec((None, HEAD_GROUP, C_HEAD_DIM, C_HEAD_DIM), lambda b, h, c: (b, h, 0, 0))
    o, s_new = pl.pallas_call(
        functools.partial(_delta_kernel, nc=nc),
        grid=(B, ng, nc),
        in_specs=[pl.BlockSpec((CHUNK, qw), lambda b, h, c: (b * nc + c, h)),
                  pl.BlockSpec((CHUNK, qw), lambda b, h, c: (b * nc + c, kb0 + h)),
                  pl.BlockSpec((CHUNK, vw), lambda b, h, c: (b * nc + c, vb0 + h)),
                  pl.BlockSpec((CHUNK, vw), lambda b, h, c: (row0 // CHUNK + b * nc + c, zb0 + h)),
                  pl.BlockSpec((None, CHUNK, LANES), lambda b, h, c: (h, b * nc + c, 0)),
                  pl.BlockSpec((None, HEAD_GROUP, CHUNK), lambda b, h, c: (b * nc + c, h, 0)),
                  st_spec,
                  pl.BlockSpec((1, C_HEAD_DIM), lambda b, h, c: (0, 0))],
        out_specs=[pl.BlockSpec((CHUNK, vw), lambda b, h, c: (b * nc + c, h)), st_spec],
        out_shape=[jax.ShapeDtypeStruct((rows, C_VAL_DIM), BF16),
                   jax.ShapeDtypeStruct(state.shape, F32)],
        scratch_shapes=[pltpu.VMEM((HEAD_GROUP, C_HEAD_DIM, C_HEAD_DIM), F32)],
        compiler_params=_cparams(("parallel", "parallel", "arbitrary")),
        name="delta_layer",
    )(qkvc, qkvc, qkvc, proj, col, rowg, state.astype(F32), norm_g.reshape(1, C_HEAD_DIM))
    return o, s_new


def _router_kernel(x_ref, w_ref, b_ref, o_ref):
    o_ref[...] = _dot(x_ref[...].astype(BF16), w_ref[...].astype(BF16)) + b_ref[...]


def router_logits(x, w_router, b_router, tm=512):
    T = x.shape[0]
    tm = _tile(T, tm)
    w = jnp.zeros((D_MODEL, LANES), F32).at[:, :N_EXPERTS].set(w_router)
    b = jnp.zeros((1, LANES), F32).at[0, :N_EXPERTS].set(b_router)
    out = pl.pallas_call(
        _router_kernel, grid=(T // tm,),
        in_specs=[pl.BlockSpec((tm, D_MODEL), lambda i: (i, 0)),
                  pl.BlockSpec((D_MODEL, LANES), lambda i: (0, 0)),
                  pl.BlockSpec((1, LANES), lambda i: (0, 0))],
        out_specs=pl.BlockSpec((tm, LANES), lambda i: (i, 0)),
        out_shape=jax.ShapeDtypeStruct((T, LANES), F32),
        compiler_params=_cparams(("parallel",)), name="router",
    )(x, w, b)
    return out[:, :N_EXPERTS]


def _rank_kernel(e_ref, rank_ref, cnt_ref, carry_ref):
    i = pl.program_id(0)
    tt = e_ref.shape[0]

    @pl.when(i == 0)
    def _():
        carry_ref[...] = jnp.zeros(carry_ref.shape, F32)

    lane = lax.broadcasted_iota(jnp.int32, (tt, LANES), 1)
    e = e_ref[...]
    oh = [jnp.where(lane == e[:, k:k + 1], 1.0, 0.0).astype(F32) for k in range(TOP_K)]
    tot = oh[0]
    for k in range(1, TOP_K):
        tot = tot + oh[k]
    r = lax.broadcasted_iota(jnp.int32, (tt, tt), 0)
    c = lax.broadcasted_iota(jnp.int32, (tt, tt), 1)
    before = _dot(jnp.where(r > c, 1.0, 0.0).astype(BF16), tot.astype(BF16)) + carry_ref[...]
    out = jnp.zeros((tt, LANES), F32)
    for k in range(TOP_K):
        rk = jnp.sum(oh[k] * before, axis=1, keepdims=True)
        out = jnp.where(lane == k, rk, out)
        before = before + oh[k]
    rank_ref[...] = out.astype(jnp.int32)
    carry_ref[...] += jnp.sum(tot, axis=0, keepdims=True)
    cnt_ref[...] = carry_ref[...].astype(jnp.int32)


def assignment_ranks(top_i, tt=512):
    T = top_i.shape[0]
    tt = _tile(T, tt)
    rank, cnt = pl.pallas_call(
        _rank_kernel, grid=(T // tt,),
        in_specs=[pl.BlockSpec((tt, TOP_K), lambda i: (i, 0))],
        out_specs=[pl.BlockSpec((tt, LANES), lambda i: (i, 0)), pl.BlockSpec((1, LANES), lambda i: (0, 0))],
        out_shape=[jax.ShapeDtypeStruct((T, LANES), jnp.int32), jax.ShapeDtypeStruct((1, LANES), jnp.int32)],
        scratch_shapes=[pltpu.VMEM((1, LANES), F32)],
        compiler_params=_cparams(("arbitrary",)), name="assignment_ranks",
    )(top_i)
    return rank[:, :TOP_K], cnt[0, :N_EXPERTS]


def _gmm1_kernel(be_ref, first_ref, nact_ref, x_ref, wg_ref, wu_ref, bg_ref, bu_ref, o_ref,
                 wgb_ref, wub_ref):
    r = pl.program_id(1)
    active = r < nact_ref[0]

    @pl.when(jnp.logical_and(active, first_ref[r] == 1))
    def _():
        wgb_ref[...] = wg_ref[...].astype(BF16)
        wub_ref[...] = wu_ref[...].astype(BF16)

    @pl.when(active)
    def _():
        x = x_ref[...].astype(BF16)
        gate = jnp.minimum(_dot(x, wgb_ref[...]) + bg_ref[...], SWIGLU_LIMIT)
        up = jnp.clip(_dot(x, wub_ref[...]) + bu_ref[...], -SWIGLU_LIMIT, SWIGLU_LIMIT)
        act = (up + 1.0) * gate * _sigmoid(SWIGLU_ALPHA * gate)
        o_ref[...] = act.astype(o_ref.dtype)

    @pl.when(jnp.logical_not(active))
    def _():
        o_ref[...] = jnp.zeros(o_ref.shape, o_ref.dtype)


def gmm_gate_up(xs, w_gu, b_gu, layer, blk_exp, first, nact, tn=1024):
    R = xs.shape[0]
    nb = R // MOE_TILE
    nj = D_FF // tn
    rmap = lambda r, na: jnp.minimum(r, na[0] - 1)
    b3 = b_gu.reshape(DEPTH, N_EXPERTS, 1, 2 * D_FF)
    grid_spec = pltpu.PrefetchScalarGridSpec(
        num_scalar_prefetch=3,
        grid=(nj, nb),
        in_specs=[pl.BlockSpec((MOE_TILE, D_MODEL), lambda j, r, be, fi, na: (rmap(r, na), 0)),
                  pl.BlockSpec((None, None, D_MODEL, tn), lambda j, r, be, fi, na: (layer, be[r], 0, j)),
                  pl.BlockSpec((None, None, D_MODEL, tn), lambda j, r, be, fi, na: (layer, be[r], 0, nj + j)),
                  pl.BlockSpec((None, None, 1, tn), lambda j, r, be, fi, na: (layer, be[r], 0, j)),
                  pl.BlockSpec((None, None, 1, tn), lambda j, r, be, fi, na: (layer, be[r], 0, nj + j))],
        out_specs=pl.BlockSpec((MOE_TILE, tn), lambda j, r, be, fi, na: (r, j)),
        scratch_shapes=[pltpu.VMEM((D_MODEL, tn), BF16)] * 2)
    return pl.pallas_call(
        _gmm1_kernel, grid_spec=grid_spec,
        out_shape=jax.ShapeDtypeStruct((R, D_FF), BF16),
        compiler_params=_cparams(("arbitrary", "arbitrary")),
        name="gmm_gate_up",
    )(blk_exp, first, nact, xs, w_gu, w_gu, b3, b3)


def _gmm2_kernel(be_ref, first_ref, nact_ref, a_ref, w_ref, b_ref, o_ref, wb_ref):
    r = pl.program_id(1)
    active = r < nact_ref[0]

    @pl.when(jnp.logical_and(active, first_ref[r] == 1))
    def _():
        wb_ref[...] = w_ref[...].astype(BF16)

    @pl.when(active)
    def _():
        o_ref[...] = _dot(a_ref[...], wb_ref[...]) + b_ref[...]

    @pl.when(jnp.logical_not(active))
    def _():
        o_ref[...] = jnp.zeros(o_ref.shape, o_ref.dtype)


def gmm_down(act, w_down, b_down, layer, blk_exp, first, nact, tn=2048):
    R = act.shape[0]
    nb = R // MOE_TILE
    rmap = lambda r, na: jnp.minimum(r, na[0] - 1)
    b3 = b_down.reshape(DEPTH, N_EXPERTS, 1, D_MODEL)
    grid_spec = pltpu.PrefetchScalarGridSpec(
        num_scalar_prefetch=3,
        grid=(D_MODEL // tn, nb),
        in_specs=[pl.BlockSpec((MOE_TILE, D_FF), lambda j, r, be, fi, na: (rmap(r, na), 0)),
                  pl.BlockSpec((None, None, D_FF, tn), lambda j, r, be, fi, na: (layer, be[r], 0, j)),
                  pl.BlockSpec((None, None, 1, tn), lambda j, r, be, fi, na: (layer, be[r], 0, j))],
        out_specs=pl.BlockSpec((MOE_TILE, tn), lambda j, r, be, fi, na: (r, j)),
        scratch_shapes=[pltpu.VMEM((D_FF, tn), BF16)])
    return pl.pallas_call(
        _gmm2_kernel, grid_spec=grid_spec,
        out_shape=jax.ShapeDtypeStruct((R, D_MODEL), F32),
        compiler_params=_cparams(("arbitrary", "arbitrary")),
        name="gmm_down",
    )(blk_exp, first, nact, act, w_down, b3)


def _combine_ln_kernel(ys_ref, gt_ref, x_ref, g_ref, b_ref, of_ref, ob_ref):
    f = ys_ref[0] * gt_ref[:, 0:1]
    for k in range(1, TOP_K):
        f = f + ys_ref[k] * gt_ref[:, k:k + 1]
    y = _layer_norm(DN_ALPHA * x_ref[...] + f, g_ref[...], b_ref[...])
    of_ref[...] = y
    ob_ref[...] = y.astype(BF16)


def combine_ln(ys_tok, gates, x, g, b, tm=256):
    T = x.shape[0]
    tm = _tile(T, tm)
    vec = pl.BlockSpec((1, D_MODEL), lambda i: (0, 0))
    row = pl.BlockSpec((tm, D_MODEL), lambda i: (i, 0))
    return pl.pallas_call(
        _combine_ln_kernel, grid=(T // tm,),
        in_specs=[pl.BlockSpec((TOP_K, tm, D_MODEL), lambda i: (0, i, 0)),
                  pl.BlockSpec((tm, TOP_K), lambda i: (i, 0)),
                  row, vec, vec],
        out_specs=[row, row],
        out_shape=[jax.ShapeDtypeStruct((T, D_MODEL), F32), jax.ShapeDtypeStruct((T, D_MODEL), BF16)],
        compiler_params=_cparams(("parallel",)), name="combine_ln",
    )(ys_tok, gates, x, g.reshape(1, -1), b.reshape(1, -1))


def moe_layer(x, layer, w_router, b_router, w_gu, b_gu, w_down, b_down, g, b):
    T = x.shape[0]
    logits = router_logits(x, w_router, b_router)
    top_v, top_i = lax.top_k(logits, TOP_K)
    gates = jax.nn.softmax(top_v, axis=-1)
    n_as = T * TOP_K
    top_i = top_i.astype(jnp.int32)
    rank, counts = assignment_ranks(top_i)
    start = jnp.cumsum(counts) - counts
    padded = (counts + MOE_TILE - 1) // MOE_TILE * MOE_TILE
    pad_end = jnp.cumsum(padded)
    pad_start = pad_end - padded
    pos = jnp.take(pad_start, top_i, mode="clip") + rank
    n_blocks = -(-n_as // MOE_TILE) + N_EXPERTS
    blk_exp = jnp.minimum(jnp.searchsorted(pad_end, jnp.arange(n_blocks, dtype=jnp.int32) * MOE_TILE,
                                           side='right'), N_EXPERTS - 1).astype(jnp.int32)
    nact = (pad_end[-1] // MOE_TILE).astype(jnp.int32).reshape(1)
    first = jnp.concatenate([jnp.ones((1,), jnp.int32),
                             (blk_exp[1:] != blk_exp[:-1]).astype(jnp.int32)])
    order = jnp.argsort(top_i.reshape(n_as))
    row_e = jnp.repeat(blk_exp, MOE_TILE)
    idx = jnp.arange(n_blocks * MOE_TILE, dtype=jnp.int32) - jnp.take(pad_start, row_e, mode="clip")
    src = jnp.take(start, row_e, mode="clip") + idx
    valid = idx < jnp.take(counts, row_e, mode="clip")
    row_tok = jnp.where(valid, jnp.take(order, src, mode="clip") // TOP_K, 0).astype(jnp.int32)
    xs = jnp.take(x, row_tok, axis=0, mode="clip")
    act = gmm_gate_up(xs, w_gu, b_gu, layer, blk_exp, first, nact)
    ys = gmm_down(act, w_down, b_down, layer, blk_exp, first, nact)
    ys_tok = jnp.take(ys, pos.T, axis=0, mode="clip")
    return combine_ln(ys_tok, gates, x, g, b)


def kernel(x_prompt, x_sample, cache_a_k, cache_a_v, cache_pool, cache_conv, state_delta, cache_mem_k,
           cache_mem_v, mem_prompt, ln_g, ln_b, w_qkv_a, w_o_a, lam_q1, lam_k1, lam_q2, lam_k2, subln_g,
           w_pool, pool_scale, w_in_c, conv_w_c, a_log_c, dt_bias_c, norm_g_c, w_o_c, w_q_m, w_kv_m,
           w_o_m, w_router, b_router, w_gu, b_gu, w_down, b_down):
    Bp, Lp, _ = x_prompt.shape
    Bs, Ls, _ = x_sample.shape
    past = cache_a_k.shape[2]
    Tp, Ts = Bp * Lp, Bs * Ls
    M = mem_prompt.shape[1]

    x = jnp.concatenate([x_prompt.reshape(Tp, D_MODEL), x_sample.reshape(Ts, D_MODEL)], 0)
    xb = x.astype(BF16)
    pos_all = jnp.concatenate([jnp.tile(jnp.arange(Lp, dtype=jnp.int32), Bp),
                               jnp.tile(past + jnp.arange(Ls, dtype=jnp.int32), Bs)])
    tabs = rope_tables(pos_all)

    mem_b = mem_prompt.reshape(Bp * M, D_MODEL).astype(BF16)
    mem_kv = [matmul(mem_b, w_kv_m[i].astype(BF16), tm=Bp * M) for i in range(DEPTH)]
    ck = cache_mem_k.reshape(DEPTH, Bs, M, D_MODEL)
    cv = cache_mem_v.reshape(DEPTH, Bs, M, D_MODEL)
    cak = cache_a_k.reshape(cache_a_k.shape[0], Bs, past, D_MODEL)
    cav = cache_a_v.reshape(cache_a_v.shape[0], Bs, past, D_MODEL)

    new_k, new_v, new_pool, new_conv, new_delta = [], [], [], [], []
    for i in range(DEPTH):
        m, j = i % N_MIXERS, i // N_MIXERS
        g0, b0 = ln_g[i, 0], ln_b[i, 0]
        if m == 0:
            qkv = matmul_rope(xb, w_qkv_a[j].astype(BF16), tabs, 2 * D_MODEL)
            new_k.append(qkv[:, D_MODEL:2 * D_MODEL])
            new_v.append(qkv[:, 2 * D_MODEL:])
            lam_init = 0.8 - 0.6 * math.exp(-0.3 * i)
            lamv = jnp.stack([lam_q1[j], lam_k1[j], lam_q2[j], lam_k2[j]]).astype(F32)
            o_p = flash_prompt(qkv, Bp, Lp, lamv, subln_g[j], lam_init)
            o_s = attn_sample(qkv, Tp, Bs, Ls, cak, cav, j, lamv, subln_g[j], lam_init)
            x, xb = matmul_res_ln(jnp.concatenate([o_p, o_s], 0), w_o_a[j].astype(BF16), x, g0, b0)
        elif m == 1:
            new_pool.append(x)
            wpb = w_pool[j].astype(BF16)
            xp, xpb = pool_layer(x, 0, Bp, Lp, 0, jnp.zeros((Bp, POOL_HIST, D_MODEL), F32),
                                 wpb, pool_scale[j], g0, b0)
            xs_, xsb = pool_layer(x, Tp, Bs, Ls, past, cache_pool[j], wpb, pool_scale[j], g0, b0)
            x, xb = jnp.concatenate([xp, xs_], 0), jnp.concatenate([xpb, xsb], 0)
        else:
            n_main = C_CONV_DIM + C_VAL_DIM
            w_in = w_in_c[j]
            proj = matmul(xb, w_in[:, :n_main].astype(BF16))
            w_ba = jnp.zeros((D_MODEL, LANES), BF16).at[:, :2 * C_V_HEADS].set(w_in[:, n_main:].astype(BF16))
            ba = matmul(xb, w_ba, tn=LANES)
            new_conv.append(proj)
            gates = gate_layer(ba, a_log_c[j], dt_bias_c[j])
            outs = []
            for (row0, B, L, hist, st) in (
                    (0, Bp, Lp, jnp.zeros((Bp, C_CONV - 1, C_CONV_DIM), F32),
                     jnp.zeros((Bp, C_V_HEADS, C_HEAD_DIM, C_HEAD_DIM), F32)),
                    (Tp, Bs, Ls, cache_conv[j], state_delta[j])):
                qkvc = conv_layer(proj, row0, B, L, hist, conv_w_c[j])
                o, s_new = delta_layer(qkvc, proj, gates, row0, B, L, st, norm_g_c[j])
                outs.append(o)
                new_delta.append(s_new)
            x, xb = matmul_res_ln(jnp.concatenate(outs, 0), w_o_c[j].astype(BF16), x, g0, b0)

        wq = w_q_m[i].astype(BF16)
        kv = mem_kv[i]
        c_p = mem_attn(xb, wq, 0, Bp, Lp, kv, kv,
                       pl.BlockSpec((M, D_MODEL), lambda b, r: (b, 0)),
                       pl.BlockSpec((M, D_MODEL), lambda b, r: (b, 1)))
        cache_spec = pl.BlockSpec((None, None, M, D_MODEL), lambda b, r, i=i: (i, b, 0, 0))
        c_s = mem_attn(xb, wq, Tp, Bs, Ls, ck, cv, cache_spec, cache_spec)
        x, xb = matmul_res_ln(jnp.concatenate([c_p, c_s], 0), w_o_m[i].astype(BF16), x,
                              ln_g[i, 1], ln_b[i, 1])

        x, xb = moe_layer(x, i, w_router[i], b_router[i], w_gu, b_gu, w_down, b_down,
                          ln_g[i, 2], ln_b[i, 2])

    def split(t, shape_p, shape_s):
        return t[:Tp].reshape(shape_p), t[Tp:].reshape(shape_s)

    y_prompt, y_sample = split(x, (Bp, Lp, D_MODEL), (Bs, Ls, D_MODEL))
    kp, ks = zip(*[split(t, (Bp, Lp, 2 * A_HEADS, A_HEAD_DIM), (Bs, Ls, 2 * A_HEADS, A_HEAD_DIM))
                   for t in new_k])
    vp, vs = zip(*[split(t, (Bp, Lp, A_HEADS, A_V_DIM), (Bs, Ls, A_HEADS, A_V_DIM)) for t in new_v])
    pp, ps = zip(*[split(t, (Bp, Lp, D_MODEL), (Bs, Ls, D_MODEL)) for t in new_pool])
    cp, cs = zip(*[split(t[:, :C_CONV_DIM], (Bp, Lp, C_CONV_DIM), (Bs, Ls, C_CONV_DIM)) for t in new_conv])
    mk = jnp.stack([kv[:, :D_MODEL].reshape(Bp, M, M_HEADS, M_HEAD_DIM) for kv in mem_kv])
    mv = jnp.stack([kv[:, D_MODEL:].reshape(Bp, M, M_HEADS, M_HEAD_DIM) for kv in mem_kv])
    return (y_prompt, y_sample, jnp.stack(kp), jnp.stack(vp), jnp.stack(ks), jnp.stack(vs),
            jnp.stack([t[:, -POOL_HIST:] for t in pp]), jnp.stack([t[:, -POOL_HIST:] for t in ps]),
            jnp.stack([t[:, -(C_CONV - 1):] for t in cp]), jnp.stack([t[:, -(C_CONV - 1):] for t in cs]),
            jnp.stack(new_delta[0::2]), jnp.stack(new_delta[1::2]), mk, mv)
```

```python
import functools
import math

import jax
import jax.numpy as jnp
from jax import lax
from jax.experimental import pallas as pl
from jax.experimental.pallas import tpu as pltpu

F32 = jnp.float32
BF16 = jnp.bfloat16

D_MODEL = 2048
DEPTH = 4
CHUNK = 64
N_MIXERS = 3
A_HEADS = 16
A_HEAD_DIM = D_MODEL // (2 * A_HEADS)
A_V_DIM = 2 * A_HEAD_DIM
ROT_DIM = A_HEAD_DIM // 4
ROPE_THETA = 500000.0
POOL_WINDOWS = (2, 4, 8, 16)
POOL_GROUP_DIM = D_MODEL // len(POOL_WINDOWS)
POOL_HIST = max(POOL_WINDOWS) - 1
C_QK_HEADS = 16
C_V_HEADS = 32
C_HEAD_DIM = 128
C_KEY_DIM = C_QK_HEADS * C_HEAD_DIM
C_VAL_DIM = C_V_HEADS * C_HEAD_DIM
C_CONV_DIM = 2 * C_KEY_DIM + C_VAL_DIM
C_CONV = 4
M_HEADS = 4
M_HEAD_DIM = D_MODEL // M_HEADS
N_EXPERTS = 32
TOP_K = 4
D_FF = D_MODEL
SWIGLU_LIMIT = 7.0
SWIGLU_ALPHA = 1.702
DN_ALPHA = (2 * DEPTH) ** 0.25
LN_EPS = 1e-5

LANES = 128
SUBLANES = 8
VMEM_LIMIT = 52 * 1024 * 1024
HEAD_GROUP = 16
MOE_TILE = 256


def _cparams(sem):
    return pltpu.CompilerParams(dimension_semantics=sem, vmem_limit_bytes=VMEM_LIMIT)


def _tile(n, pref, mult=SUBLANES):
    t = min(pref, n)
    while t > mult and (n % t or t % mult):
        t -= mult
    assert n % t == 0, (n, pref)
    return t


def _carry_specs(n_inputs, carried):
    carried = tuple(carried or ())
    specs = [pl.BlockSpec(memory_space=pl.ANY)] * len(carried)
    return specs, carried, {n_inputs + k: k for k in range(len(carried))}


def _dot(a, b):
    return jnp.dot(a, b, preferred_element_type=F32)


def _dot_nt(a, b):
    return lax.dot_general(a, b, (((1,), (1,)), ((), ())), preferred_element_type=F32)


def _dot_tn(a, b):
    return lax.dot_general(a, b, (((0,), (0,)), ((), ())), preferred_element_type=F32)


def _split(a):
    hi = a.astype(BF16)
    lo = (a - hi.astype(F32)).astype(BF16)
    return hi, lo


def _dot3(a, b):
    ah, al = _split(a)
    bh, bl = _split(b)
    return _dot(ah, bh) + _dot(ah, bl) + _dot(al, bh)


def _sigmoid(x):
    return 1.0 / (1.0 + jnp.exp(-x))


def _layer_norm(y, g, b):
    mu = jnp.mean(y, axis=-1, keepdims=True)
    d = y - mu
    var = jnp.mean(d * d, axis=-1, keepdims=True)
    return d * lax.rsqrt(var + LN_EPS) * g + b


def _mm_kernel(a_ref, w_ref, o_ref):
    o_ref[...] = _dot(a_ref[...], w_ref[...]).astype(o_ref.dtype)


def matmul(a, w, out_dtype=F32, tm=512, tn=1024):
    M, K = a.shape
    N = w.shape[1]
    tm, tn = _tile(M, tm), _tile(N, tn, LANES)
    return pl.pallas_call(
        _mm_kernel,
        grid=(N // tn, M // tm),
        in_specs=[pl.BlockSpec((tm, K), lambda j, i: (i, 0)),
                  pl.BlockSpec((K, tn), lambda j, i: (0, j))],
        out_specs=pl.BlockSpec((tm, tn), lambda j, i: (i, j)),
        out_shape=jax.ShapeDtypeStruct((M, N), out_dtype),
        compiler_params=_cparams(("parallel", "parallel")),
        name="matmul",
    )(a, w)


def _mm_rope_kernel(a_ref, w_ref, c_ref, s1_ref, s2_ref, o_ref, *, n_rope, tn):
    j = pl.program_id(0)
    acc = _dot(a_ref[...], w_ref[...])

    @pl.when(j < n_rope)
    def _():
        c, s1, s2 = c_ref[...], s1_ref[...], s2_ref[...]
        half = ROT_DIM // 2
        for cb in range(tn // LANES):
            x = acc[:, cb * LANES:(cb + 1) * LANES]
            o_ref[:, cb * LANES:(cb + 1) * LANES] = (
                x * c + pltpu.roll(x, half, 1) * s1 + pltpu.roll(x, LANES - half, 1) * s2)

    @pl.when(j >= n_rope)
    def _():
        o_ref[...] = acc


def matmul_rope(a, w, tabs, n_rope_cols, tm=512, tn=1024):
    M, K = a.shape
    N = w.shape[1]
    tm, tn = _tile(M, tm), _tile(N, tn, LANES)
    tab_spec = pl.BlockSpec((tm, LANES), lambda j, i: (i, 0))
    return pl.pallas_call(
        functools.partial(_mm_rope_kernel, n_rope=n_rope_cols // tn, tn=tn),
        grid=(N // tn, M // tm),
        in_specs=[pl.BlockSpec((tm, K), lambda j, i: (i, 0)),
                  pl.BlockSpec((K, tn), lambda j, i: (0, j)),
                  tab_spec, tab_spec, tab_spec],
        out_specs=pl.BlockSpec((tm, tn), lambda j, i: (i, j)),
        out_shape=jax.ShapeDtypeStruct((M, N), F32),
        compiler_params=_cparams(("parallel", "parallel")),
        name="matmul_rope",
    )(a, w, *tabs)


def rope_tables(pos):
    inv = ROPE_THETA ** (-jnp.arange(0, ROT_DIM, 2, dtype=F32) / ROT_DIM)
    ang = pos.astype(F32)[:, None] * inv[None, :]
    cos, sin = jnp.cos(ang), jnp.sin(ang)
    half = ROT_DIM // 2
    ones = jnp.ones((pos.shape[0], A_HEAD_DIM - ROT_DIM), F32)
    zeros = jnp.zeros((pos.shape[0], A_HEAD_DIM - ROT_DIM), F32)
    zh = jnp.zeros((pos.shape[0], half), F32)
    c = jnp.concatenate([cos, cos, ones], 1)
    s1 = jnp.concatenate([zh, sin, zeros], 1)
    s2 = jnp.concatenate([-sin, zh, zeros], 1)
    rep = LANES // A_HEAD_DIM
    return tuple(jnp.tile(t, (1, rep)) for t in (c, s1, s2))


def _mm_res_ln_kernel(a_ref, w_ref, x_ref, g_ref, b_ref, of_ref, ob_ref, acc_ref, *, nk):
    k = pl.program_id(1)

    def finish(h):
        y = _layer_norm(DN_ALPHA * x_ref[...] + h, g_ref[...], b_ref[...])
        of_ref[...] = y
        ob_ref[...] = y.astype(BF16)

    part = _dot(a_ref[...], w_ref[...])
    if nk == 1:
        finish(part)
    else:
        @pl.when(k == 0)
        def _():
            acc_ref[...] = part

        @pl.when(jnp.logical_and(k > 0, k < nk - 1))
        def _():
            acc_ref[...] += part

        @pl.when(k == nk - 1)
        def _():
            finish(acc_ref[...] + part)


def matmul_res_ln(a, w, x, g, b, tm=256, tk=2048):
    M, K = a.shape
    N = w.shape[1]
    tm, tk = _tile(M, tm), _tile(K, tk, LANES)
    nk = K // tk
    vec = pl.BlockSpec((1, N), lambda i, k: (0, 0))
    row = pl.BlockSpec((tm, N), lambda i, k: (i, 0))
    return pl.pallas_call(
        functools.partial(_mm_res_ln_kernel, nk=nk),
        grid=(M // tm, nk),
        in_specs=[pl.BlockSpec((tm, tk), lambda i, k: (i, k)),
                  pl.BlockSpec((tk, N), lambda i, k: (k, 0)),
                  row, vec, vec],
        out_specs=[row, row],
        out_shape=[jax.ShapeDtypeStruct((M, N), F32), jax.ShapeDtypeStruct((M, N), BF16)],
        scratch_shapes=[pltpu.VMEM((tm, N), F32)],
        compiler_params=_cparams(("parallel", "arbitrary")),
        name="matmul_res_ln",
    )(a, w, x, g.reshape(1, N), b.reshape(1, N))


def _attn_init(m_ref, l_ref, acc_ref):
    m_ref[...] = jnp.full(m_ref.shape, -jnp.inf, F32)
    l_ref[...] = jnp.zeros(l_ref.shape, F32)
    acc_ref[...] = jnp.zeros(acc_ref.shape, F32)


def _attn_step(q_ref, k_ref, v_ref, m_ref, l_ref, acc_ref, n_heads, mask, rows=None, cols=None):
    rows = rows or slice(0, q_ref.shape[0])
    cols = cols or slice(0, k_ref.shape[0])
    m_ref, l_ref, acc_ref = m_ref.at[:, rows], l_ref.at[:, rows], acc_ref.at[:, rows]
    tq = rows.stop - rows.start
    lane = lax.broadcasted_iota(jnp.int32, (tq, LANES), 1)
    first = lane < A_HEAD_DIM
    hsl = lambda h: slice(h * LANES, (h + 1) * LANES)
    maps = [(h, mp) for h in range(n_heads) for mp in range(2)]
    q = [q_ref[rows, hsl(h)] * (A_HEAD_DIM ** -0.5) for h in range(n_heads)]
    k = [k_ref[cols, hsl(h)].astype(BF16) for h in range(n_heads)]
    v = [v_ref[cols, hsl(h)].astype(BF16) for h in range(n_heads)]
    s = [_dot_nt(jnp.where(first if mp == 0 else jnp.logical_not(first), q[h], 0.0).astype(BF16), k[h])
         for h, mp in maps]
    if mask is not None:
        s = [jnp.where(mask, x, -jnp.inf) for x in s]
    m_prev = [m_ref[i] for i in range(len(maps))]
    m_new = [jnp.maximum(m_prev[i], jnp.max(s[i], axis=1, keepdims=True)) for i in range(len(maps))]
    alpha = [jnp.exp(m_prev[i] - m_new[i]) for i in range(len(maps))]
    p = [jnp.exp(s[i] - m_new[i][:, :1]) for i in range(len(maps))]
    for i in range(len(maps)):
        l_ref[i] = alpha[i] * l_ref[i] + jnp.sum(p[i], axis=1, keepdims=True)
        m_ref[i] = m_new[i]
    pv = [_dot(p[i].astype(BF16), v[h]) for i, (h, mp) in enumerate(maps)]
    for i in range(len(maps)):
        acc_ref[i] = alpha[i] * acc_ref[i] + pv[i]


def _attn_finish(lamv_ref, g_ref, o_ref, l_ref, acc_ref, n_heads, lam_init):
    lamv = lamv_ref[...]
    lam = (jnp.exp(jnp.sum(lamv[0:1] * lamv[1:2], axis=1, keepdims=True))
           - jnp.exp(jnp.sum(lamv[2:3] * lamv[3:4], axis=1, keepdims=True)) + lam_init)
    for h in range(n_heads):
        o = acc_ref[2 * h] / l_ref[2 * h] - lam * (acc_ref[2 * h + 1] / l_ref[2 * h + 1])
        ms = jnp.mean(o * o, axis=1, keepdims=True)
        y = o * lax.rsqrt(ms + 1e-5) * g_ref[...] * (1.0 - lam_init)
        o_ref[:, h * LANES:(h + 1) * LANES] = y.astype(o_ref.dtype)


def _flash_prompt_kernel(qt_ref, kt_ref, q_ref, k_ref, v_ref, lamv_ref, g_ref, o_ref,
                         m_ref, l_ref, acc_ref, *, hpb, lam_init):
    p = pl.program_id(2)
    qb, kb = qt_ref[p], kt_ref[p]
    tq = q_ref.shape[0]

    @pl.when(kb == 0)
    def _():
        _attn_init(m_ref, l_ref, acc_ref)

    @pl.when(kb < qb)
    def _():
        _attn_step(q_ref, k_ref, v_ref, m_ref, l_ref, acc_ref, hpb, None)

    @pl.when(kb == qb)
    def _():
        half = tq // 2
        for rows, ncols in ((slice(0, half), half), (slice(half, tq), tq)):
            r = (rows.start + lax.broadcasted_iota(jnp.int32, (half, ncols), 0)) // CHUNK
            c = lax.broadcasted_iota(jnp.int32, (half, ncols), 1) // CHUNK
            _attn_step(q_ref, k_ref, v_ref, m_ref, l_ref, acc_ref, hpb, r >= c, rows, slice(0, ncols))
        _attn_finish(lamv_ref, g_ref, o_ref, l_ref, acc_ref, hpb, lam_init)


def flash_prompt(qkv, B, L, lamv, g, lam_init, tq=1024, hpb=1):
    tq = _tile(L, tq, CHUNK)
    nq = L // tq
    pairs = [(qb, kb) for qb in range(nq) for kb in range(qb + 1)]
    qt = jnp.asarray([p[0] for p in pairs], jnp.int32)
    kt = jnp.asarray([p[1] for p in pairs], jnp.int32)
    w = hpb * LANES
    hb = D_MODEL // w
    grid_spec = pltpu.PrefetchScalarGridSpec(
        num_scalar_prefetch=2,
        grid=(B, A_HEADS // hpb, len(pairs)),
        in_specs=[pl.BlockSpec((tq, w), lambda b, h, p, qt, kt: (b * nq + qt[p], h)),
                  pl.BlockSpec((tq, w), lambda b, h, p, qt, kt: (b * nq + kt[p], hb + h)),
                  pl.BlockSpec((tq, w), lambda b, h, p, qt, kt: (b * nq + kt[p], 2 * hb + h)),
                  pl.BlockSpec((4, A_HEAD_DIM), lambda b, h, p, qt, kt: (0, 0)),
                  pl.BlockSpec((1, LANES), lambda b, h, p, qt, kt: (0, 0))],
        out_specs=pl.BlockSpec((tq, w), lambda b, h, p, qt, kt: (b * nq + qt[p], h)),
        scratch_shapes=[pltpu.VMEM((2 * hpb, tq, LANES), F32)] * 3)
    return pl.pallas_call(
        functools.partial(_flash_prompt_kernel, hpb=hpb, lam_init=lam_init),
        grid_spec=grid_spec,
        out_shape=jax.ShapeDtypeStruct((qkv.shape[0], D_MODEL), BF16),
        compiler_params=_cparams(("parallel", "parallel", "arbitrary")),
        name="flash_prompt",
    )(qt, kt, qkv, qkv, qkv, lamv, g.reshape(1, LANES))


def _attn_sample_kernel(q_ref, kh_ref, vh_ref, kn_ref, vn_ref, lamv_ref, g_ref, *rest, hpb, lam_init):
    o_ref = rest[-1]
    tq = q_ref.shape[0]
    lane = lax.broadcasted_iota(jnp.int32, (tq, LANES), 1)
    first = lane < A_HEAD_DIM
    hsl = lambda h: slice(h * LANES, (h + 1) * LANES)
    maps = [(h, mp) for h in range(hpb) for mp in range(2)]
    lamv = lamv_ref[...]
    lam = (jnp.exp(jnp.sum(lamv[0:1] * lamv[1:2], axis=1, keepdims=True))
           - jnp.exp(jnp.sum(lamv[2:3] * lamv[3:4], axis=1, keepdims=True)) + lam_init)
    q = [q_ref[:, hsl(h)] * (A_HEAD_DIM ** -0.5) for h in range(hpb)]
    qm = [jnp.where(first if mp == 0 else jnp.logical_not(first), q[h], 0.0).astype(BF16) for h, mp in maps]
    sh = [_dot_nt(qm[i], kh_ref[:, hsl(h)].astype(BF16)) for i, (h, mp) in enumerate(maps)]
    sn = [_dot_nt(qm[i], kn_ref[:, hsl(h)].astype(BF16)) for i, (h, mp) in enumerate(maps)]
    m = [jnp.maximum(jnp.max(sh[i], axis=1, keepdims=True), jnp.max(sn[i], axis=1, keepdims=True))
         for i in range(len(maps))]
    ph = [jnp.exp(sh[i] - m[i]) for i in range(len(maps))]
    pn = [jnp.exp(sn[i] - m[i]) for i in range(len(maps))]
    l = [jnp.sum(ph[i], axis=1, keepdims=True) + jnp.sum(pn[i], axis=1, keepdims=True) for i in range(len(maps))]
    for h in range(hpb):
        wh = ph[2 * h] / l[2 * h] - lam * (ph[2 * h + 1] / l[2 * h + 1])
        wn = pn[2 * h] / l[2 * h] - lam * (pn[2 * h + 1] / l[2 * h + 1])
        o = (_dot(wh.astype(BF16), vh_ref[:, hsl(h)].astype(BF16))
             + _dot(wn.astype(BF16), vn_ref[:, hsl(h)].astype(BF16)))
        ms = jnp.mean(o * o, axis=1, keepdims=True)
        y = o * lax.rsqrt(ms + 1e-5) * g_ref[...] * (1.0 - lam_init)
        o_ref[:, hsl(h)] = y.astype(o_ref.dtype)


def attn_sample(qkv, row0, B, L, k_cache, v_cache, j, lamv, g, lam_init, carried, hpb=4):
    past = k_cache.shape[2]
    assert L == CHUNK and past % CHUNK == 0 and row0 % L == 0
    rb = row0 // L
    w = hpb * LANES
    hb = D_MODEL // w
    new = lambda c: pl.BlockSpec((L, w), lambda b, h: (rb + b, c * hb + h))
    hist = pl.BlockSpec((None, None, past, w), lambda b, h: (j, b, 0, h))
    c_specs, c_args, aliases = _carry_specs(7, carried)
    return pl.pallas_call(
        functools.partial(_attn_sample_kernel, hpb=hpb, lam_init=lam_init),
        grid=(B, hb),
        in_specs=[new(0), hist, hist, new(1), new(2),
                  pl.BlockSpec((4, A_HEAD_DIM), lambda b, h: (0, 0)),
                  pl.BlockSpec((1, LANES), lambda b, h: (0, 0))] + c_specs,
        out_specs=new(0),
        out_shape=jax.ShapeDtypeStruct((qkv.shape[0], D_MODEL), BF16),
        input_output_aliases=aliases,
        compiler_params=_cparams(("parallel", "parallel")),
        name="attn_sample",
    )(qkv, k_cache, v_cache, qkv, qkv, lamv, g.reshape(1, LANES), *c_args)


def _mem_attn_kernel(x_ref, wq_ref, mk_ref, mv_ref, *rest):
    o_ref = rest[-1]
    q = _dot(x_ref[...], wq_ref[...])
    for h in range(M_HEADS):
        sl = slice(h * M_HEAD_DIM, (h + 1) * M_HEAD_DIM)
        s = _dot_nt(q[:, sl].astype(BF16), mk_ref[:, sl].astype(BF16)) * (M_HEAD_DIM ** -0.5)
        p = jnp.exp(s - jnp.max(s, axis=1, keepdims=True))
        p = p / jnp.sum(p, axis=1, keepdims=True)
        o_ref[:, sl] = _dot(p.astype(BF16), mv_ref[:, sl].astype(BF16)).astype(o_ref.dtype)


def mem_attn(xb, wq, row0, B, L, k_arr, v_arr, k_spec, v_spec, carried=None, tm=512):
    tm = _tile(L, tm)
    nb = L // tm
    rb = row0 // tm
    assert row0 % tm == 0
    row = pl.BlockSpec((tm, D_MODEL), lambda b, i: (rb + b * nb + i, 0))
    c_specs, c_args, aliases = _carry_specs(4, carried)
    return pl.pallas_call(
        _mem_attn_kernel,
        grid=(B, nb),
        in_specs=[row, pl.BlockSpec((D_MODEL, D_MODEL), lambda b, i: (0, 0)), k_spec, v_spec] + c_specs,
        out_specs=row,
        out_shape=jax.ShapeDtypeStruct(xb.shape, BF16),
        input_output_aliases=aliases,
        compiler_params=_cparams(("parallel", "parallel")),
        name="mem_attn",
    )(xb, wq, k_arr, v_arr, *c_args)


def _pool_kernel(x_ref, prev_ref, hist_ref, w_ref, sc_ref, g_ref, b_ref, *rest, tr, pos0):
    of_ref, ob_ref, cat_ref, y_ref = rest[-4:]
    i = pl.program_id(1)
    H = POOL_HIST + 1
    cat_ref[0:H, :] = jnp.where(i == 0, hist_ref[...], prev_ref[...])
    cat_ref[H:, :] = x_ref[...]
    pos = pos0 + i * tr + lax.broadcasted_iota(jnp.int32, (tr, 1), 0)
    for gi, w in enumerate(POOL_WINDOWS):
        cs = slice(gi * POOL_GROUP_DIM, (gi + 1) * POOL_GROUP_DIM)
        cur = x_ref[:, cs]
        win = cur
        for jj in range(1, w):
            win = win + cat_ref[H - jj:H - jj + tr, cs]
        cnt = jnp.minimum(pos + 1, w).astype(F32)
        pooled = win / cnt - cur
        y_ref[:, cs] = _dot(pooled.astype(BF16), w_ref[gi]) * sc_ref[:, cs]
    y = _layer_norm(DN_ALPHA * x_ref[...] + y_ref[...], g_ref[...], b_ref[...])
    of_ref[...] = y
    ob_ref[...] = y.astype(BF16)


def pool_layer(x, row0, B, L, pos0, hist, w_pool_b, pool_scale, g, b, carried=None, tr=256):
    tr = _tile(L, tr, POOL_HIST + 1)
    H = POOL_HIST + 1
    nb = L // tr
    assert row0 % tr == 0
    hist_p = jnp.concatenate([jnp.zeros((B, 1, D_MODEL), F32), hist], 1)
    vec = pl.BlockSpec((1, D_MODEL), lambda b_, i: (0, 0))
    row = pl.BlockSpec((tr, D_MODEL), lambda b_, i: (row0 // tr + b_ * nb + i, 0))
    c_specs, c_args, aliases = _carry_specs(7, carried)
    return pl.pallas_call(
        functools.partial(_pool_kernel, tr=tr, pos0=pos0),
        grid=(B, nb),
        in_specs=[row,
                  pl.BlockSpec((H, D_MODEL),
                               lambda b_, i: (jnp.maximum((row0 + b_ * L + i * tr) // H - 1, 0), 0)),
                  pl.BlockSpec((None, H, D_MODEL), lambda b_, i: (b_, 0, 0)),
                  pl.BlockSpec((len(POOL_WINDOWS), POOL_GROUP_DIM, POOL_GROUP_DIM), lambda b_, i: (0, 0, 0)),
                  vec, vec, vec] + c_specs,
        out_specs=[row, row],
        out_shape=[jax.ShapeDtypeStruct(x.shape, F32), jax.ShapeDtypeStruct(x.shape, BF16)],
        input_output_aliases=aliases,
        scratch_shapes=[pltpu.VMEM((tr + H, D_MODEL), F32), pltpu.VMEM((tr, D_MODEL), F32)],
        compiler_params=_cparams(("parallel", "parallel")),
        name="pool_layer",
    )(x, x, hist_p, w_pool_b, pool_scale.reshape(1, -1), g.reshape(1, -1), b.reshape(1, -1), *c_args)


def _conv_kernel(cur_ref, prev_ref, hist_ref, w_ref, o_ref, cat_ref, *, tr, tc, nq, nqk):
    i = pl.program_id(1)
    c = pl.program_id(2)
    cat_ref[0:SUBLANES, :] = jnp.where(i == 0, hist_ref[...], prev_ref[...])
    cat_ref[SUBLANES:, :] = cur_ref[...]
    acc = cur_ref[...] * w_ref[C_CONV - 1:C_CONV, :]
    for jj in range(C_CONV - 1):
        s = SUBLANES - (C_CONV - 1) + jj
        acc = acc + cat_ref[s:s + tr, :] * w_ref[jj:jj + 1, :]
    y = acc * _sigmoid(acc)

    @pl.when(c < nqk)
    def _():
        scale = jnp.where(c < nq, C_HEAD_DIM ** -0.5, 1.0)
        for h in range(tc // C_HEAD_DIM):
            sl = slice(h * C_HEAD_DIM, (h + 1) * C_HEAD_DIM)
            yh = y[:, sl]
            o_ref[:, sl] = yh * (lax.rsqrt(jnp.sum(yh * yh, axis=1, keepdims=True) + 1e-6) * scale)

    @pl.when(c >= nqk)
    def _():
        o_ref[...] = y


def conv_layer(proj, row0, B, L, hist, conv_w, tr=512, tc=1024):
    tr = _tile(L, tr)
    nb = L // tr
    assert row0 % tr == 0
    hist_p = jnp.concatenate([jnp.zeros((B, SUBLANES - (C_CONV - 1), C_CONV_DIM), F32), hist], 1)
    return pl.pallas_call(
        functools.partial(_conv_kernel, tr=tr, tc=tc, nq=C_KEY_DIM // tc, nqk=2 * C_KEY_DIM // tc),
        grid=(B, nb, C_CONV_DIM // tc),
        in_specs=[pl.BlockSpec((tr, tc), lambda b, i, c: (row0 // tr + b * nb + i, c)),
                  pl.BlockSpec((SUBLANES, tc),
                               lambda b, i, c: (jnp.maximum((row0 + b * L + i * tr) // SUBLANES - 1, 0), c)),
                  pl.BlockSpec((None, SUBLANES, tc), lambda b, i, c: (b, 0, c)),
                  pl.BlockSpec((C_CONV, tc), lambda b, i, c: (0, c))],
        out_specs=pl.BlockSpec((tr, tc), lambda b, i, c: (b * nb + i, c)),
        out_shape=jax.ShapeDtypeStruct((B * L, C_CONV_DIM), F32),
        scratch_shapes=[pltpu.VMEM((tr + SUBLANES, tc), F32)],
        compiler_params=_cparams(("parallel", "parallel", "parallel")),
        name="conv_layer",
    )(proj, proj, hist_p, conv_w)


def _gate_kernel(ba_ref, al_ref, dt_ref, o_ref):
    ba = ba_ref[...]
    lane = lax.broadcasted_iota(jnp.int32, ba.shape, 1)
    x = ba + dt_ref[...]
    softplus = jnp.maximum(x, 0.0) + jnp.log(1.0 + jnp.exp(-jnp.abs(x)))
    g = jnp.where(jnp.logical_and(lane >= C_V_HEADS, lane < 2 * C_V_HEADS),
                  -jnp.exp(al_ref[...]) * softplus, 0.0)
    r = lax.broadcasted_iota(jnp.int32, (CHUNK, CHUNK), 0)
    c = lax.broadcasted_iota(jnp.int32, (CHUNK, CHUNK), 1)
    tril = jnp.where(r >= c, 1.0, 0.0).astype(F32)
    gc = _dot3(tril, g)
    o_ref[...] = jnp.where(lane < C_V_HEADS, _sigmoid(ba), gc)


def gate_layer(ba, a_log, dt_bias):
    T = ba.shape[0]
    pad = lambda v: jnp.zeros((1, LANES), F32).at[0, C_V_HEADS:2 * C_V_HEADS].set(v.astype(F32))
    vec = pl.BlockSpec((1, LANES), lambda i: (0, 0))
    blk = pl.BlockSpec((CHUNK, LANES), lambda i: (i, 0))
    return pl.pallas_call(
        _gate_kernel, grid=(T // CHUNK,), in_specs=[blk, vec, vec], out_specs=blk,
        out_shape=jax.ShapeDtypeStruct((T, LANES), F32),
        compiler_params=_cparams(("parallel",)), name="gate_layer",
    )(ba, pad(a_log), pad(dt_bias))


def _delta_kernel(q_ref, k_ref, v_ref, z_ref, col_ref, row_ref, s0_ref, ng_ref, *rest, nc):
    o_ref, sout_ref, S_ref = rest[-3:]
    c = pl.program_id(2)

    @pl.when(c == 0)
    def _():
        S_ref[...] = s0_ref[...]

    col = col_ref[...]
    rowg = row_ref[...]
    r = lax.broadcasted_iota(jnp.int32, (CHUNK, CHUNK), 0)
    cc = lax.broadcasted_iota(jnp.int32, (CHUNK, CHUNK), 1)
    incl, strict = r >= cc, r > cc
    eye = jnp.where(r == cc, 1.0, 0.0).astype(F32)
    rep = C_V_HEADS // C_QK_HEADS
    heads = range(HEAD_GROUP)
    hsl = lambda h: slice(h * C_HEAD_DIM, (h + 1) * C_HEAD_DIM)
    kh = [k_ref[:, hsl(h)] for h in range(HEAD_GROUP // rep)]
    khb = [k.astype(BF16) for k in kh]
    gcc = [col[:, h:h + 1] for h in heads]
    beta = [col[:, HEAD_GROUP + h:HEAD_GROUP + h + 1] for h in heads]
    decay = [jnp.where(incl, jnp.exp(jnp.where(incl, gcc[h] - rowg[h:h + 1, :], 0.0)), 0.0) for h in heads]
    kb = [kh[h // rep] * beta[h] for h in heads]
    kk = [_dot_nt(kb[h].astype(BF16), khb[h // rep]) for h in heads]
    qk = [_dot_nt(q_ref[:, hsl(h // rep)].astype(BF16), khb[h // rep]) for h in heads]
    npow = [jnp.where(strict, -kk[h] * decay[h], 0.0) for h in heads]
    t_inv = [eye + npow[h] for h in heads]
    for _ in range(int(math.log2(CHUNK)) - 1):
        npow = [_dot3(n, n) for n in npow]
        t_inv = [t_inv[h] + _dot3(t_inv[h], npow[h]) for h in heads]
    tb = [t.astype(BF16) for t in t_inv]
    egc = [jnp.exp(g) for g in gcc]
    u = [_dot(tb[h], (v_ref[:, hsl(h)] * beta[h]).astype(BF16)) for h in heads]
    w = [_dot(tb[h], (kb[h] * egc[h]).astype(BF16)) for h in heads]
    S = [S_ref[h] for h in heads]
    Sb = [s.astype(BF16) for s in S]
    vnb = [(u[h] - _dot(w[h].astype(BF16), Sb[h])).astype(BF16) for h in heads]
    a_qk = [jnp.where(incl, qk[h] * decay[h], 0.0).astype(BF16) for h in heads]
    o = [_dot((q_ref[:, hsl(h // rep)] * egc[h]).astype(BF16), Sb[h]) + _dot(a_qk[h], vnb[h]) for h in heads]
    for h in heads:
        gl = gcc[h][CHUNK - 1:CHUNK, :]
        kg = kh[h // rep] * jnp.exp(gl - gcc[h])
        S_ref[h] = S[h] * jnp.exp(gl) + _dot_tn(kg.astype(BF16), vnb[h])
    for h in heads:
        zz = z_ref[:, hsl(h)]
        ms = jnp.mean(o[h] * o[h], axis=1, keepdims=True)
        y = o[h] * lax.rsqrt(ms + 1e-6) * ng_ref[...] * (zz * _sigmoid(zz))
        o_ref[:, hsl(h)] = y.astype(o_ref.dtype)

    @pl.when(c == nc - 1)
    def _():
        sout_ref[...] = S_ref[...]


def delta_layer(qkvc, proj, gates, row0, B, L, state, norm_g, carried=None):
    nc = L // CHUNK
    ng = C_V_HEADS // HEAD_GROUP
    rows = B * L
    assert L % CHUNK == 0 and row0 % CHUNK == 0
    gt = lax.slice_in_dim(gates, row0, row0 + rows, axis=0)
    beta = gt[:, :C_V_HEADS].reshape(rows, ng, HEAD_GROUP)
    gc = gt[:, C_V_HEADS:2 * C_V_HEADS].reshape(rows, ng, HEAD_GROUP)
    col = jnp.concatenate([gc, beta, jnp.zeros((rows, ng, LANES - 2 * HEAD_GROUP), F32)], -1)
    col = col.transpose(1, 0, 2)
    rowg = gt[:, C_V_HEADS:2 * C_V_HEADS].reshape(B * nc, CHUNK, C_V_HEADS).transpose(0, 2, 1)
    qw = HEAD_GROUP * C_HEAD_DIM * C_QK_HEADS // C_V_HEADS
    vw = HEAD_GROUP * C_HEAD_DIM
    kb0 = C_KEY_DIM // qw
    vb0 = 2 * C_KEY_DIM // vw
    zb0 = C_CONV_DIM // vw
    st_spec = pl.BlockSpec((None, HEAD_GROUP, C_HEAD_DIM, C_HEAD_DIM), lambda b, h, c: (b, h, 0, 0))
    c_specs, c_args, aliases = _carry_specs(8, carried)
    o, s_new = pl.pallas_call(
        functools.partial(_delta_kernel, nc=nc),
        grid=(B, ng, nc),
        in_specs=[pl.BlockSpec((CHUNK, qw), lambda b, h, c: (b * nc + c, h)),
                  pl.BlockSpec((CHUNK, qw), lambda b, h, c: (b * nc + c, kb0 + h)),
                  pl.BlockSpec((CHUNK, vw), lambda b, h, c: (b * nc + c, vb0 + h)),
                  pl.BlockSpec((CHUNK, vw), lambda b, h, c: (row0 // CHUNK + b * nc + c, zb0 + h)),
                  pl.BlockSpec((None, CHUNK, LANES), lambda b, h, c: (h, b * nc + c, 0)),
                  pl.BlockSpec((None, HEAD_GROUP, CHUNK), lambda b, h, c: (b * nc + c, h, 0)),
                  st_spec,
                  pl.BlockSpec((1, C_HEAD_DIM), lambda b, h, c: (0, 0))] + c_specs,
        out_specs=[pl.BlockSpec((CHUNK, vw), lambda b, h, c: (row0 // CHUNK + b * nc + c, h)), st_spec],
        out_shape=[jax.ShapeDtypeStruct((proj.shape[0], C_VAL_DIM), BF16),
                   jax.ShapeDtypeStruct(state.shape, F32)],
        input_output_aliases=aliases,
        scratch_shapes=[pltpu.VMEM((HEAD_GROUP, C_HEAD_DIM, C_HEAD_DIM), F32)],
        compiler_params=_cparams(("parallel", "parallel", "arbitrary")),
        name="delta_layer",
    )(qkvc, qkvc, qkvc, proj, col, rowg, state.astype(F32), norm_g.reshape(1, C_HEAD_DIM), *c_args)
    return o, s_new


def _route_kernel(x_ref, w_ref, b_ref, sel_ref, gate_ref, rank_ref, cnt_ref, carry_ref, *, n_precise):
    i = pl.program_id(0)
    tm = x_ref.shape[0]

    @pl.when(i == 0)
    def _():
        carry_ref[...] = jnp.zeros(carry_ref.shape, F32)

    lane_i = lax.broadcasted_iota(jnp.int32, (tm, LANES), 1)
    lane = lane_i.astype(F32)
    x, w = x_ref[...], w_ref[...]
    logits = jnp.where(i < n_precise, _dot3(x, w), _dot(x.astype(BF16), w.astype(BF16))) + b_ref[...]
    v = jnp.where(lane_i < N_EXPERTS, logits, -jnp.inf)
    oh, vals = [], []
    sel = jnp.zeros((tm, LANES), F32)
    for k in range(TOP_K):
        mk = jnp.max(v, axis=1, keepdims=True)
        ik = jnp.min(jnp.where(v == mk, lane, float(LANES)), axis=1, keepdims=True)
        hit = lane == ik
        oh.append(jnp.where(hit, 1.0, 0.0).astype(F32))
        vals.append(mk)
        sel = jnp.where(lane_i == k, ik, sel)
        v = jnp.where(hit, -jnp.inf, v)
    ex = [jnp.exp(vals[k] - vals[0]) for k in range(TOP_K)]
    den = ex[0]
    for k in range(1, TOP_K):
        den = den + ex[k]
    gate = jnp.zeros((tm, LANES), F32)
    for k in range(TOP_K):
        gate = jnp.where(lane_i == k, ex[k] / den, gate)
    tot = oh[0]
    for k in range(1, TOP_K):
        tot = tot + oh[k]
    r = lax.broadcasted_iota(jnp.int32, (tm, tm), 0)
    c = lax.broadcasted_iota(jnp.int32, (tm, tm), 1)
    before = _dot(jnp.where(r > c, 1.0, 0.0).astype(BF16), tot.astype(BF16)) + carry_ref[...]
    rank = jnp.zeros((tm, LANES), F32)
    for k in range(TOP_K):
        rank = jnp.where(lane_i == k, jnp.sum(oh[k] * before, axis=1, keepdims=True), rank)
        before = before + oh[k]
    sel_ref[...] = sel.astype(jnp.int32)
    gate_ref[...] = gate
    rank_ref[...] = rank.astype(jnp.int32)
    carry_ref[...] += jnp.sum(tot, axis=0, keepdims=True)
    cnt_ref[...] = carry_ref[...].astype(jnp.int32)


def route(x, w_router, b_router, precise_rows, tm=512):
    T = x.shape[0]
    tm = _tile(T, tm)
    w = jnp.zeros((D_MODEL, LANES), F32).at[:, :N_EXPERTS].set(w_router)
    b = jnp.zeros((1, LANES), F32).at[0, :N_EXPERTS].set(b_router)
    blk = pl.BlockSpec((tm, LANES), lambda i: (i, 0))
    vec = pl.BlockSpec((1, LANES), lambda i: (0, 0))
    sel, gate, rank, cnt = pl.pallas_call(
        functools.partial(_route_kernel, n_precise=precise_rows // tm), grid=(T // tm,),
        in_specs=[pl.BlockSpec((tm, D_MODEL), lambda i: (i, 0)),
                  pl.BlockSpec((D_MODEL, LANES), lambda i: (0, 0)), vec],
        out_specs=[blk, blk, blk, vec],
        out_shape=[jax.ShapeDtypeStruct((T, LANES), jnp.int32), jax.ShapeDtypeStruct((T, LANES), F32),
                   jax.ShapeDtypeStruct((T, LANES), jnp.int32), jax.ShapeDtypeStruct((1, LANES), jnp.int32)],
        scratch_shapes=[pltpu.VMEM((1, LANES), F32)],
        compiler_params=_cparams(("arbitrary",)), name="route",
    )(x, w, b)
    return sel[:, :TOP_K], gate[:, :TOP_K], rank[:, :TOP_K], cnt[0, :N_EXPERTS]


def _gmm1_kernel(be_ref, first_ref, nact_ref, x_ref, wg_ref, wu_ref, bg_ref, bu_ref, o_ref,
                 wgb_ref, wub_ref):
    r = pl.program_id(1)
    active = r < nact_ref[0]

    @pl.when(jnp.logical_and(active, first_ref[r] == 1))
    def _():
        wgb_ref[...] = wg_ref[...].astype(BF16)
        wub_ref[...] = wu_ref[...].astype(BF16)

    @pl.when(active)
    def _():
        x = x_ref[...].astype(BF16)
        gate = jnp.minimum(_dot(x, wgb_ref[...]) + bg_ref[...], SWIGLU_LIMIT)
        up = jnp.clip(_dot(x, wub_ref[...]) + bu_ref[...], -SWIGLU_LIMIT, SWIGLU_LIMIT)
        act = (up + 1.0) * gate * _sigmoid(SWIGLU_ALPHA * gate)
        o_ref[...] = act.astype(o_ref.dtype)

    @pl.when(jnp.logical_not(active))
    def _():
        o_ref[...] = jnp.zeros(o_ref.shape, o_ref.dtype)


def gmm_gate_up(xs, w_gu, b_gu, layer, blk_exp, first, nact, tn=1024):
    R = xs.shape[0]
    nb = R // MOE_TILE
    nj = D_FF // tn
    rmap = lambda r, na: jnp.minimum(r, na[0] - 1)
    b3 = b_gu.reshape(DEPTH, N_EXPERTS, 1, 2 * D_FF)
    grid_spec = pltpu.PrefetchScalarGridSpec(
        num_scalar_prefetch=3,
        grid=(nj, nb),
        in_specs=[pl.BlockSpec((MOE_TILE, D_MODEL), lambda j, r, be, fi, na: (rmap(r, na), 0)),
                  pl.BlockSpec((None, None, D_MODEL, tn), lambda j, r, be, fi, na: (layer, be[r], 0, j)),
                  pl.BlockSpec((None, None, D_MODEL, tn), lambda j, r, be, fi, na: (layer, be[r], 0, nj + j)),
                  pl.BlockSpec((None, None, 1, tn), lambda j, r, be, fi, na: (layer, be[r], 0, j)),
                  pl.BlockSpec((None, None, 1, tn), lambda j, r, be, fi, na: (layer, be[r], 0, nj + j))],
        out_specs=pl.BlockSpec((MOE_TILE, tn), lambda j, r, be, fi, na: (r, j)),
        scratch_shapes=[pltpu.VMEM((D_MODEL, tn), BF16)] * 2)
    return pl.pallas_call(
        _gmm1_kernel, grid_spec=grid_spec,
        out_shape=jax.ShapeDtypeStruct((R, D_FF), BF16),
        compiler_params=_cparams(("arbitrary", "arbitrary")),
        name="gmm_gate_up",
    )(blk_exp, first, nact, xs, w_gu, w_gu, b3, b3)


def _gmm2_kernel(be_ref, first_ref, nact_ref, a_ref, w_ref, b_ref, o_ref, wb_ref):
    r = pl.program_id(1)
    active = r < nact_ref[0]

    @pl.when(jnp.logical_and(active, first_ref[r] == 1))
    def _():
        wb_ref[...] = w_ref[...].astype(BF16)

    @pl.when(active)
    def _():
        o_ref[...] = _dot(a_ref[...], wb_ref[...]) + b_ref[...]

    @pl.when(jnp.logical_not(active))
    def _():
        o_ref[...] = jnp.zeros(o_ref.shape, o_ref.dtype)


def gmm_down(act, w_down, b_down, layer, blk_exp, first, nact, tn=2048):
    R = act.shape[0]
    nb = R // MOE_TILE
    rmap = lambda r, na: jnp.minimum(r, na[0] - 1)
    b3 = b_down.reshape(DEPTH, N_EXPERTS, 1, D_MODEL)
    grid_spec = pltpu.PrefetchScalarGridSpec(
        num_scalar_prefetch=3,
        grid=(D_MODEL // tn, nb),
        in_specs=[pl.BlockSpec((MOE_TILE, D_FF), lambda j, r, be, fi, na: (rmap(r, na), 0)),
                  pl.BlockSpec((None, None, D_FF, tn), lambda j, r, be, fi, na: (layer, be[r], 0, j)),
                  pl.BlockSpec((None, None, 1, tn), lambda j, r, be, fi, na: (layer, be[r], 0, j))],
        out_specs=pl.BlockSpec((MOE_TILE, tn), lambda j, r, be, fi, na: (r, j)),
        scratch_shapes=[pltpu.VMEM((D_FF, tn), BF16)])
    return pl.pallas_call(
        _gmm2_kernel, grid_spec=grid_spec,
        out_shape=jax.ShapeDtypeStruct((R, D_MODEL), F32),
        compiler_params=_cparams(("arbitrary", "arbitrary")),
        name="gmm_down",
    )(blk_exp, first, nact, act, w_down, b3)


def _combine_ln_kernel(ys_ref, gt_ref, x_ref, g_ref, b_ref, of_ref, ob_ref):
    f = ys_ref[0] * gt_ref[:, 0:1]
    for k in range(1, TOP_K):
        f = f + ys_ref[k] * gt_ref[:, k:k + 1]
    y = _layer_norm(DN_ALPHA * x_ref[...] + f, g_ref[...], b_ref[...])
    of_ref[...] = y
    ob_ref[...] = y.astype(BF16)


def combine_ln(ys_tok, gates, x, g, b, tm=256):
    T = x.shape[0]
    tm = _tile(T, tm)
    vec = pl.BlockSpec((1, D_MODEL), lambda i: (0, 0))
    row = pl.BlockSpec((tm, D_MODEL), lambda i: (i, 0))
    return pl.pallas_call(
        _combine_ln_kernel, grid=(T // tm,),
        in_specs=[pl.BlockSpec((TOP_K, tm, D_MODEL), lambda i: (0, i, 0)),
                  pl.BlockSpec((tm, TOP_K), lambda i: (i, 0)),
                  row, vec, vec],
        out_specs=[row, row],
        out_shape=[jax.ShapeDtypeStruct((T, D_MODEL), F32), jax.ShapeDtypeStruct((T, D_MODEL), BF16)],
        compiler_params=_cparams(("parallel",)), name="combine_ln",
    )(ys_tok, gates, x, g.reshape(1, -1), b.reshape(1, -1))


def moe_layer(x, precise_rows, layer, w_router, b_router, w_gu, b_gu, w_down, b_down, g, b):
    T = x.shape[0]
    top_i, gates, rank, counts = route(x, w_router, b_router, precise_rows)
    n_as = T * TOP_K
    padded = (counts + MOE_TILE - 1) // MOE_TILE * MOE_TILE
    pad_end = jnp.cumsum(padded)
    pad_start = pad_end - padded
    pos = jnp.take(pad_start, top_i, mode="clip") + rank
    n_blocks = -(-n_as // MOE_TILE) + N_EXPERTS
    blk_exp = jnp.minimum(jnp.searchsorted(pad_end, jnp.arange(n_blocks, dtype=jnp.int32) * MOE_TILE,
                                           side='right'), N_EXPERTS - 1).astype(jnp.int32)
    nact = (pad_end[-1] // MOE_TILE).astype(jnp.int32).reshape(1)
    first = jnp.concatenate([jnp.ones((1,), jnp.int32),
                             (blk_exp[1:] != blk_exp[:-1]).astype(jnp.int32)])
    row_tok = jnp.zeros((n_blocks * MOE_TILE,), jnp.int32).at[pos.reshape(n_as)].set(
        jnp.arange(n_as, dtype=jnp.int32) // TOP_K, unique_indices=True)
    xs = jnp.take(x, row_tok, axis=0, mode="clip")
    act = gmm_gate_up(xs, w_gu, b_gu, layer, blk_exp, first, nact)
    ys = gmm_down(act, w_down, b_down, layer, blk_exp, first, nact)
    ys_tok = jnp.take(ys, pos.T, axis=0, mode="clip")
    return combine_ln(ys_tok, gates, x, g, b)


def kernel(x_prompt, x_sample, cache_a_k, cache_a_v, cache_pool, cache_conv, state_delta, cache_mem_k,
           cache_mem_v, mem_prompt, ln_g, ln_b, w_qkv_a, w_o_a, lam_q1, lam_k1, lam_q2, lam_k2, subln_g,
           w_pool, pool_scale, w_in_c, conv_w_c, a_log_c, dt_bias_c, norm_g_c, w_o_c, w_q_m, w_kv_m,
           w_o_m, w_router, b_router, w_gu, b_gu, w_down, b_down):
    Bp, Lp, _ = x_prompt.shape
    Bs, Ls, _ = x_sample.shape
    past = cache_a_k.shape[2]
    Tp, Ts = Bp * Lp, Bs * Ls
    M = mem_prompt.shape[1]

    x = jnp.concatenate([x_prompt.reshape(Tp, D_MODEL), x_sample.reshape(Ts, D_MODEL)], 0)
    xb = x.astype(BF16)
    pos_all = jnp.concatenate([jnp.tile(jnp.arange(Lp, dtype=jnp.int32), Bp),
                               jnp.tile(past + jnp.arange(Ls, dtype=jnp.int32), Bs)])
    tabs = rope_tables(pos_all)

    mem_b = mem_prompt.reshape(Bp * M, D_MODEL).astype(BF16)
    mem_kv = [matmul(mem_b, w_kv_m[i].astype(BF16), tm=Bp * M) for i in range(DEPTH)]
    ck = cache_mem_k.reshape(DEPTH, Bs, M, D_MODEL)
    cv = cache_mem_v.reshape(DEPTH, Bs, M, D_MODEL)
    cak = cache_a_k.reshape(cache_a_k.shape[0], Bs, past, D_MODEL)
    cav = cache_a_v.reshape(cache_a_v.shape[0], Bs, past, D_MODEL)

    new_k, new_v, new_pool, new_conv, new_delta = [], [], [], [], []
    for i in range(DEPTH):
        m, j = i % N_MIXERS, i // N_MIXERS
        g0, b0 = ln_g[i, 0], ln_b[i, 0]
        if m == 0:
            qkv = matmul_rope(xb, w_qkv_a[j].astype(BF16), tabs, 2 * D_MODEL)
            new_k.append(qkv[:, D_MODEL:2 * D_MODEL])
            new_v.append(qkv[:, 2 * D_MODEL:])
            lam_init = 0.8 - 0.6 * math.exp(-0.3 * i)
            lamv = jnp.stack([lam_q1[j], lam_k1[j], lam_q2[j], lam_k2[j]]).astype(F32)
            o = flash_prompt(qkv, Bp, Lp, lamv, subln_g[j], lam_init)
            o = attn_sample(qkv, Tp, Bs, Ls, cak, cav, j, lamv, subln_g[j], lam_init, carried=(o,))
            x, xb = matmul_res_ln(o, w_o_a[j].astype(BF16), x, g0, b0)
        elif m == 1:
            new_pool.append(x)
            wpb = w_pool[j].astype(BF16)
            pooled = pool_layer(x, 0, Bp, Lp, 0, jnp.zeros((Bp, POOL_HIST, D_MODEL), F32),
                                wpb, pool_scale[j], g0, b0)
            x, xb = pool_layer(x, Tp, Bs, Ls, past, cache_pool[j], wpb, pool_scale[j], g0, b0, carried=pooled)
        else:
            n_main = C_CONV_DIM + C_VAL_DIM
            w_in = w_in_c[j]
            proj = matmul(xb, w_in[:, :n_main].astype(BF16))
            w_ba = jnp.zeros((D_MODEL, LANES), BF16).at[:, :2 * C_V_HEADS].set(w_in[:, n_main:].astype(BF16))
            ba = matmul(xb, w_ba, tn=LANES)
            new_conv.append(proj)
            gates = gate_layer(ba, a_log_c[j], dt_bias_c[j])
            o = None
            for (row0, B, L, hist, st) in (
                    (0, Bp, Lp, jnp.zeros((Bp, C_CONV - 1, C_CONV_DIM), F32),
                     jnp.zeros((Bp, C_V_HEADS, C_HEAD_DIM, C_HEAD_DIM), F32)),
                    (Tp, Bs, Ls, cache_conv[j], state_delta[j])):
                qkvc = conv_layer(proj, row0, B, L, hist, conv_w_c[j])
                o, s_new = delta_layer(qkvc, proj, gates, row0, B, L, st, norm_g_c[j],
                                       carried=None if o is None else (o,))
                new_delta.append(s_new)
            x, xb = matmul_res_ln(o, w_o_c[j].astype(BF16), x, g0, b0)

        wq = w_q_m[i].astype(BF16)
        kv = mem_kv[i]
        c = mem_attn(xb, wq, 0, Bp, Lp, kv, kv,
                     pl.BlockSpec((M, D_MODEL), lambda b, r: (b, 0)),
                     pl.BlockSpec((M, D_MODEL), lambda b, r: (b, 1)))
        cache_spec = pl.BlockSpec((None, None, M, D_MODEL), lambda b, r, i=i: (i, b, 0, 0))
        c = mem_attn(xb, wq, Tp, Bs, Ls, ck, cv, cache_spec, cache_spec, carried=(c,))
        x, xb = matmul_res_ln(c, w_o_m[i].astype(BF16), x, ln_g[i, 1], ln_b[i, 1])

        x, xb = moe_layer(x, Tp, i, w_router[i], b_router[i], w_gu, b_gu, w_down, b_down,
                          ln_g[i, 2], ln_b[i, 2])

    def split(t, shape_p, shape_s):
        return t[:Tp].reshape(shape_p), t[Tp:].reshape(shape_s)

    y_prompt, y_sample = split(x, (Bp, Lp, D_MODEL), (Bs, Ls, D_MODEL))
    kp, ks = zip(*[split(t, (Bp, Lp, 2 * A_HEADS, A_HEAD_DIM), (Bs, Ls, 2 * A_HEADS, A_HEAD_DIM))
                   for t in new_k])
    vp, vs = zip(*[split(t, (Bp, Lp, A_HEADS, A_V_DIM), (Bs, Ls, A_HEADS, A_V_DIM)) for t in new_v])
    pp, ps = zip(*[split(t, (Bp, Lp, D_MODEL), (Bs, Ls, D_MODEL)) for t in new_pool])
    cp, cs = zip(*[split(t[:, :C_CONV_DIM], (Bp, Lp, C_CONV_DIM), (Bs, Ls, C_CONV_DIM)) for t in new_conv])
    mk = jnp.stack([kv[:, :D_MODEL].reshape(Bp, M, M_HEADS, M_HEAD_DIM) for kv in mem_kv])
    mv = jnp.stack([kv[:, D_MODEL:].reshape(Bp, M, M_HEADS, M_HEAD_DIM) for kv in mem_kv])
    return (y_prompt, y_sample, jnp.stack(kp), jnp.stack(vp), jnp.stack(ks), jnp.stack(vs),
            jnp.stack([t[:, -POOL_HIST:] for t in pp]), jnp.stack([t[:, -POOL_HIST:] for t in ps]),
            jnp.stack([t[:, -(C_CONV - 1):] for t in cp]), jnp.stack([t[:, -(C_CONV - 1):] for t in cs]),
            jnp.stack(new_delta[0::2]), jnp.stack(new_delta[1::2]), mk, mv)
```

```python
import functools
import math

import jax
import jax.numpy as jnp
from jax import lax
from jax.experimental import pallas as pl
from jax.experimental.pallas import tpu as pltpu

F32 = jnp.float32
BF16 = jnp.bfloat16

D_MODEL = 2048
DEPTH = 4
CHUNK = 64
N_MIXERS = 3
A_HEADS = 16
A_HEAD_DIM = D_MODEL // (2 * A_HEADS)
A_V_DIM = 2 * A_HEAD_DIM
ROT_DIM = A_HEAD_DIM // 4
ROPE_THETA = 500000.0
POOL_WINDOWS = (2, 4, 8, 16)
POOL_GROUP_DIM = D_MODEL // len(POOL_WINDOWS)
POOL_HIST = max(POOL_WINDOWS) - 1
C_QK_HEADS = 16
C_V_HEADS = 32
C_HEAD_DIM = 128
C_KEY_DIM = C_QK_HEADS * C_HEAD_DIM
C_VAL_DIM = C_V_HEADS * C_HEAD_DIM
C_CONV_DIM = 2 * C_KEY_DIM + C_VAL_DIM
C_CONV = 4
M_HEADS = 4
M_HEAD_DIM = D_MODEL // M_HEADS
N_EXPERTS = 32
TOP_K = 4
D_FF = D_MODEL
SWIGLU_LIMIT = 7.0
SWIGLU_ALPHA = 1.702
DN_ALPHA = (2 * DEPTH) ** 0.25
LN_EPS = 1e-5

LANES = 128
SUBLANES = 8
VMEM_LIMIT = 52 * 1024 * 1024
HEAD_GROUP = 16
MOE_TILE = 256


def _cparams(sem):
    return pltpu.CompilerParams(dimension_semantics=sem, vmem_limit_bytes=VMEM_LIMIT)


def _tile(n, pref, mult=SUBLANES):
    t = min(pref, n)
    while t > mult and (n % t or t % mult):
        t -= mult
    assert n % t == 0, (n, pref)
    return t


def _carry_specs(n_inputs, carried):
    carried = tuple(carried or ())
    specs = [pl.BlockSpec(memory_space=pl.ANY)] * len(carried)
    return specs, carried, {n_inputs + k: k for k in range(len(carried))}


def _dot(a, b):
    return jnp.dot(a, b, preferred_element_type=F32)


def _dot_nt(a, b):
    return lax.dot_general(a, b, (((1,), (1,)), ((), ())), preferred_element_type=F32)


def _dot_tn(a, b):
    return lax.dot_general(a, b, (((0,), (0,)), ((), ())), preferred_element_type=F32)


def _split(a):
    hi = a.astype(BF16)
    lo = (a - hi.astype(F32)).astype(BF16)
    return hi, lo


def _dot3(a, b):
    ah, al = _split(a)
    bh, bl = _split(b)
    return _dot(ah, bh) + _dot(ah, bl) + _dot(al, bh)


def _sigmoid(x):
    return 1.0 / (1.0 + jnp.exp(-x))


def _layer_norm(y, g, b):
    mu = jnp.mean(y, axis=-1, keepdims=True)
    d = y - mu
    var = jnp.mean(d * d, axis=-1, keepdims=True)
    return d * lax.rsqrt(var + LN_EPS) * g + b


def _mm_kernel(a_ref, w_ref, o_ref):
    o_ref[...] = _dot(a_ref[...], w_ref[...]).astype(o_ref.dtype)


def matmul(a, w, out_dtype=F32, tm=512, tn=1024):
    M, K = a.shape
    N = w.shape[1]
    tm, tn = _tile(M, tm), _tile(N, tn, LANES)
    return pl.pallas_call(
        _mm_kernel,
        grid=(N // tn, M // tm),
        in_specs=[pl.BlockSpec((tm, K), lambda j, i: (i, 0)),
                  pl.BlockSpec((K, tn), lambda j, i: (0, j))],
        out_specs=pl.BlockSpec((tm, tn), lambda j, i: (i, j)),
        out_shape=jax.ShapeDtypeStruct((M, N), out_dtype),
        compiler_params=_cparams(("parallel", "parallel")),
        name="matmul",
    )(a, w)


def _mm_rope_kernel(a_ref, w_ref, c_ref, s1_ref, s2_ref, o_ref, *, n_rope, tn):
    j = pl.program_id(0)
    acc = _dot(a_ref[...], w_ref[...])

    @pl.when(j < n_rope)
    def _():
        c, s1, s2 = c_ref[...], s1_ref[...], s2_ref[...]
        half = ROT_DIM // 2
        for cb in range(tn // LANES):
            x = acc[:, cb * LANES:(cb + 1) * LANES]
            o_ref[:, cb * LANES:(cb + 1) * LANES] = (
                x * c + pltpu.roll(x, half, 1) * s1 + pltpu.roll(x, LANES - half, 1) * s2)

    @pl.when(j >= n_rope)
    def _():
        o_ref[...] = acc


def matmul_rope(a, w, tabs, n_rope_cols, tm=512, tn=1024):
    M, K = a.shape
    N = w.shape[1]
    tm, tn = _tile(M, tm), _tile(N, tn, LANES)
    tab_spec = pl.BlockSpec((tm, LANES), lambda j, i: (i, 0))
    return pl.pallas_call(
        functools.partial(_mm_rope_kernel, n_rope=n_rope_cols // tn, tn=tn),
        grid=(N // tn, M // tm),
        in_specs=[pl.BlockSpec((tm, K), lambda j, i: (i, 0)),
                  pl.BlockSpec((K, tn), lambda j, i: (0, j)),
                  tab_spec, tab_spec, tab_spec],
        out_specs=pl.BlockSpec((tm, tn), lambda j, i: (i, j)),
        out_shape=jax.ShapeDtypeStruct((M, N), F32),
        compiler_params=_cparams(("parallel", "parallel")),
        name="matmul_rope",
    )(a, w, *tabs)


def rope_tables(pos):
    inv = ROPE_THETA ** (-jnp.arange(0, ROT_DIM, 2, dtype=F32) / ROT_DIM)
    ang = pos.astype(F32)[:, None] * inv[None, :]
    cos, sin = jnp.cos(ang), jnp.sin(ang)
    half = ROT_DIM // 2
    ones = jnp.ones((pos.shape[0], A_HEAD_DIM - ROT_DIM), F32)
    zeros = jnp.zeros((pos.shape[0], A_HEAD_DIM - ROT_DIM), F32)
    zh = jnp.zeros((pos.shape[0], half), F32)
    c = jnp.concatenate([cos, cos, ones], 1)
    s1 = jnp.concatenate([zh, sin, zeros], 1)
    s2 = jnp.concatenate([-sin, zh, zeros], 1)
    rep = LANES // A_HEAD_DIM
    return tuple(jnp.tile(t, (1, rep)) for t in (c, s1, s2))


def _mm_res_ln_kernel(a_ref, w_ref, x_ref, g_ref, b_ref, of_ref, ob_ref, acc_ref, *, nk):
    k = pl.program_id(1)

    def finish(h):
        y = _layer_norm(DN_ALPHA * x_ref[...] + h, g_ref[...], b_ref[...])
        of_ref[...] = y
        ob_ref[...] = y.astype(BF16)

    part = _dot(a_ref[...], w_ref[...])
    if nk == 1:
        finish(part)
    else:
        @pl.when(k == 0)
        def _():
            acc_ref[...] = part

        @pl.when(jnp.logical_and(k > 0, k < nk - 1))
        def _():
            acc_ref[...] += part

        @pl.when(k == nk - 1)
        def _():
            finish(acc_ref[...] + part)


def matmul_res_ln(a, w, x, g, b, tm=256, tk=2048):
    M, K = a.shape
    N = w.shape[1]
    tm, tk = _tile(M, tm), _tile(K, tk, LANES)
    nk = K // tk
    vec = pl.BlockSpec((1, N), lambda i, k: (0, 0))
    row = pl.BlockSpec((tm, N), lambda i, k: (i, 0))
    return pl.pallas_call(
        functools.partial(_mm_res_ln_kernel, nk=nk),
        grid=(M // tm, nk),
        in_specs=[pl.BlockSpec((tm, tk), lambda i, k: (i, k)),
                  pl.BlockSpec((tk, N), lambda i, k: (k, 0)),
                  row, vec, vec],
        out_specs=[row, row],
        out_shape=[jax.ShapeDtypeStruct((M, N), F32), jax.ShapeDtypeStruct((M, N), BF16)],
        scratch_shapes=[pltpu.VMEM((tm, N), F32)],
        compiler_params=_cparams(("parallel", "arbitrary")),
        name="matmul_res_ln",
    )(a, w, x, g.reshape(1, N), b.reshape(1, N))


def _attn_init(m_ref, l_ref, acc_ref):
    m_ref[...] = jnp.full(m_ref.shape, -jnp.inf, F32)
    l_ref[...] = jnp.zeros(l_ref.shape, F32)
    acc_ref[...] = jnp.zeros(acc_ref.shape, F32)


def _attn_step(q_ref, k_ref, v_ref, m_ref, l_ref, acc_ref, n_heads, mask, rows=None, cols=None):
    rows = rows or slice(0, q_ref.shape[0])
    cols = cols or slice(0, k_ref.shape[0])
    m_ref, l_ref, acc_ref = m_ref.at[:, rows], l_ref.at[:, rows], acc_ref.at[:, rows]
    tq = rows.stop - rows.start
    lane = lax.broadcasted_iota(jnp.int32, (tq, LANES), 1)
    first = lane < A_HEAD_DIM
    hsl = lambda h: slice(h * LANES, (h + 1) * LANES)
    maps = [(h, mp) for h in range(n_heads) for mp in range(2)]
    q = [q_ref[rows, hsl(h)] * (A_HEAD_DIM ** -0.5) for h in range(n_heads)]
    k = [k_ref[cols, hsl(h)].astype(BF16) for h in range(n_heads)]
    v = [v_ref[cols, hsl(h)].astype(BF16) for h in range(n_heads)]
    s = [_dot_nt(jnp.where(first if mp == 0 else jnp.logical_not(first), q[h], 0.0).astype(BF16), k[h])
         for h, mp in maps]
    if mask is not None:
        s = [jnp.where(mask, x, -jnp.inf) for x in s]
    m_prev = [m_ref[i] for i in range(len(maps))]
    m_new = [jnp.maximum(m_prev[i], jnp.max(s[i], axis=1, keepdims=True)) for i in range(len(maps))]
    alpha = [jnp.exp(m_prev[i] - m_new[i]) for i in range(len(maps))]
    p = [jnp.exp(s[i] - m_new[i][:, :1]) for i in range(len(maps))]
    for i in range(len(maps)):
        l_ref[i] = alpha[i] * l_ref[i] + jnp.sum(p[i], axis=1, keepdims=True)
        m_ref[i] = m_new[i]
    pv = [_dot(p[i].astype(BF16), v[h]) for i, (h, mp) in enumerate(maps)]
    for i in range(len(maps)):
        acc_ref[i] = alpha[i] * acc_ref[i] + pv[i]


def _attn_finish(lamv_ref, g_ref, o_ref, l_ref, acc_ref, n_heads, lam_init):
    lamv = lamv_ref[...]
    lam = (jnp.exp(jnp.sum(lamv[0:1] * lamv[1:2], axis=1, keepdims=True))
           - jnp.exp(jnp.sum(lamv[2:3] * lamv[3:4], axis=1, keepdims=True)) + lam_init)
    for h in range(n_heads):
        o = acc_ref[2 * h] / l_ref[2 * h] - lam * (acc_ref[2 * h + 1] / l_ref[2 * h + 1])
        ms = jnp.mean(o * o, axis=1, keepdims=True)
        y = o * lax.rsqrt(ms + 1e-5) * g_ref[...] * (1.0 - lam_init)
        o_ref[:, h * LANES:(h + 1) * LANES] = y.astype(o_ref.dtype)


def _flash_prompt_kernel(qt_ref, kt_ref, q_ref, k_ref, v_ref, lamv_ref, g_ref, o_ref,
                         m_ref, l_ref, acc_ref, *, hpb, lam_init):
    p = pl.program_id(2)
    qb, kb = qt_ref[p], kt_ref[p]
    tq = q_ref.shape[0]

    @pl.when(kb == 0)
    def _():
        _attn_init(m_ref, l_ref, acc_ref)

    @pl.when(kb < qb)
    def _():
        _attn_step(q_ref, k_ref, v_ref, m_ref, l_ref, acc_ref, hpb, None)

    @pl.when(kb == qb)
    def _():
        half = tq // 2
        for rows, ncols in ((slice(0, half), half), (slice(half, tq), tq)):
            r = (rows.start + lax.broadcasted_iota(jnp.int32, (half, ncols), 0)) // CHUNK
            c = lax.broadcasted_iota(jnp.int32, (half, ncols), 1) // CHUNK
            _attn_step(q_ref, k_ref, v_ref, m_ref, l_ref, acc_ref, hpb, r >= c, rows, slice(0, ncols))
        _attn_finish(lamv_ref, g_ref, o_ref, l_ref, acc_ref, hpb, lam_init)


def flash_prompt(qkv, B, L, lamv, g, lam_init, tq=1024, hpb=1):
    tq = _tile(L, tq, CHUNK)
    nq = L // tq
    pairs = [(qb, kb) for qb in range(nq) for kb in range(qb + 1)]
    qt = jnp.asarray([p[0] for p in pairs], jnp.int32)
    kt = jnp.asarray([p[1] for p in pairs], jnp.int32)
    w = hpb * LANES
    hb = D_MODEL // w
    grid_spec = pltpu.PrefetchScalarGridSpec(
        num_scalar_prefetch=2,
        grid=(B, A_HEADS // hpb, len(pairs)),
        in_specs=[pl.BlockSpec((tq, w), lambda b, h, p, qt, kt: (b * nq + qt[p], h)),
                  pl.BlockSpec((tq, w), lambda b, h, p, qt, kt: (b * nq + kt[p], hb + h)),
                  pl.BlockSpec((tq, w), lambda b, h, p, qt, kt: (b * nq + kt[p], 2 * hb + h)),
                  pl.BlockSpec((4, A_HEAD_DIM), lambda b, h, p, qt, kt: (0, 0)),
                  pl.BlockSpec((1, LANES), lambda b, h, p, qt, kt: (0, 0))],
        out_specs=pl.BlockSpec((tq, w), lambda b, h, p, qt, kt: (b * nq + qt[p], h)),
        scratch_shapes=[pltpu.VMEM((2 * hpb, tq, LANES), F32)] * 3)
    return pl.pallas_call(
        functools.partial(_flash_prompt_kernel, hpb=hpb, lam_init=lam_init),
        grid_spec=grid_spec,
        out_shape=jax.ShapeDtypeStruct((qkv.shape[0], D_MODEL), BF16),
        compiler_params=_cparams(("parallel", "parallel", "arbitrary")),
        name="flash_prompt",
    )(qt, kt, qkv, qkv, qkv, lamv, g.reshape(1, LANES))


def _attn_sample_kernel(q_ref, kh_ref, vh_ref, kn_ref, vn_ref, lamv_ref, g_ref, *rest, hpb, lam_init):
    o_ref = rest[-1]
    tq = q_ref.shape[0]
    lane = lax.broadcasted_iota(jnp.int32, (tq, LANES), 1)
    first = lane < A_HEAD_DIM
    hsl = lambda h: slice(h * LANES, (h + 1) * LANES)
    maps = [(h, mp) for h in range(hpb) for mp in range(2)]
    lamv = lamv_ref[...]
    lam = (jnp.exp(jnp.sum(lamv[0:1] * lamv[1:2], axis=1, keepdims=True))
           - jnp.exp(jnp.sum(lamv[2:3] * lamv[3:4], axis=1, keepdims=True)) + lam_init)
    q = [q_ref[:, hsl(h)] * (A_HEAD_DIM ** -0.5) for h in range(hpb)]
    qm = [jnp.where(first if mp == 0 else jnp.logical_not(first), q[h], 0.0).astype(BF16) for h, mp in maps]
    sh = [_dot_nt(qm[i], kh_ref[:, hsl(h)].astype(BF16)) for i, (h, mp) in enumerate(maps)]
    sn = [_dot_nt(qm[i], kn_ref[:, hsl(h)].astype(BF16)) for i, (h, mp) in enumerate(maps)]
    m = [jnp.maximum(jnp.max(sh[i], axis=1, keepdims=True), jnp.max(sn[i], axis=1, keepdims=True))
         for i in range(len(maps))]
    ph = [jnp.exp(sh[i] - m[i]) for i in range(len(maps))]
    pn = [jnp.exp(sn[i] - m[i]) for i in range(len(maps))]
    l = [jnp.sum(ph[i], axis=1, keepdims=True) + jnp.sum(pn[i], axis=1, keepdims=True) for i in range(len(maps))]
    for h in range(hpb):
        wh = ph[2 * h] / l[2 * h] - lam * (ph[2 * h + 1] / l[2 * h + 1])
        wn = pn[2 * h] / l[2 * h] - lam * (pn[2 * h + 1] / l[2 * h + 1])
        o = (_dot(wh.astype(BF16), vh_ref[:, hsl(h)].astype(BF16))
             + _dot(wn.astype(BF16), vn_ref[:, hsl(h)].astype(BF16)))
        ms = jnp.mean(o * o, axis=1, keepdims=True)
        y = o * lax.rsqrt(ms + 1e-5) * g_ref[...] * (1.0 - lam_init)
        o_ref[:, hsl(h)] = y.astype(o_ref.dtype)


def attn_sample(qkv, row0, B, L, k_cache, v_cache, j, lamv, g, lam_init, carried, hpb=4):
    past = k_cache.shape[2]
    assert L == CHUNK and past % CHUNK == 0 and row0 % L == 0
    rb = row0 // L
    w = hpb * LANES
    hb = D_MODEL // w
    new = lambda c: pl.BlockSpec((L, w), lambda b, h: (rb + b, c * hb + h))
    hist = pl.BlockSpec((None, None, past, w), lambda b, h: (j, b, 0, h))
    c_specs, c_args, aliases = _carry_specs(7, carried)
    return pl.pallas_call(
        functools.partial(_attn_sample_kernel, hpb=hpb, lam_init=lam_init),
        grid=(B, hb),
        in_specs=[new(0), hist, hist, new(1), new(2),
                  pl.BlockSpec((4, A_HEAD_DIM), lambda b, h: (0, 0)),
                  pl.BlockSpec((1, LANES), lambda b, h: (0, 0))] + c_specs,
        out_specs=new(0),
        out_shape=jax.ShapeDtypeStruct((qkv.shape[0], D_MODEL), BF16),
        input_output_aliases=aliases,
        compiler_params=_cparams(("parallel", "parallel")),
        name="attn_sample",
    )(qkv, k_cache, v_cache, qkv, qkv, lamv, g.reshape(1, LANES), *c_args)


def _mem_attn_kernel(x_ref, wq_ref, mk_ref, mv_ref, *rest):
    o_ref = rest[-1]
    q = _dot(x_ref[...], wq_ref[...])
    for h in range(M_HEADS):
        sl = slice(h * M_HEAD_DIM, (h + 1) * M_HEAD_DIM)
        s = _dot_nt(q[:, sl].astype(BF16), mk_ref[:, sl].astype(BF16)) * (M_HEAD_DIM ** -0.5)
        p = jnp.exp(s - jnp.max(s, axis=1, keepdims=True))
        p = p / jnp.sum(p, axis=1, keepdims=True)
        o_ref[:, sl] = _dot(p.astype(BF16), mv_ref[:, sl].astype(BF16)).astype(o_ref.dtype)


def mem_attn(xb, wq, row0, B, L, k_arr, v_arr, k_spec, v_spec, carried=None, tm=512):
    tm = _tile(L, tm)
    nb = L // tm
    rb = row0 // tm
    assert row0 % tm == 0
    row = pl.BlockSpec((tm, D_MODEL), lambda b, i: (rb + b * nb + i, 0))
    c_specs, c_args, aliases = _carry_specs(4, carried)
    return pl.pallas_call(
        _mem_attn_kernel,
        grid=(B, nb),
        in_specs=[row, pl.BlockSpec((D_MODEL, D_MODEL), lambda b, i: (0, 0)), k_spec, v_spec] + c_specs,
        out_specs=row,
        out_shape=jax.ShapeDtypeStruct(xb.shape, BF16),
        input_output_aliases=aliases,
        compiler_params=_cparams(("parallel", "parallel")),
        name="mem_attn",
    )(xb, wq, k_arr, v_arr, *c_args)


def _pool_kernel(x_ref, prev_ref, hist_ref, w_ref, sc_ref, g_ref, b_ref, *rest, tr, pos0):
    of_ref, ob_ref, cat_ref, y_ref = rest[-4:]
    i = pl.program_id(1)
    H = POOL_HIST + 1
    cat_ref[0:H, :] = jnp.where(i == 0, hist_ref[...], prev_ref[...])
    cat_ref[H:, :] = x_ref[...]
    pos = pos0 + i * tr + lax.broadcasted_iota(jnp.int32, (tr, 1), 0)
    for gi, w in enumerate(POOL_WINDOWS):
        cs = slice(gi * POOL_GROUP_DIM, (gi + 1) * POOL_GROUP_DIM)
        cur = x_ref[:, cs]
        win = cur
        for jj in range(1, w):
            win = win + cat_ref[H - jj:H - jj + tr, cs]
        cnt = jnp.minimum(pos + 1, w).astype(F32)
        pooled = win / cnt - cur
        y_ref[:, cs] = _dot(pooled.astype(BF16), w_ref[gi]) * sc_ref[:, cs]
    y = _layer_norm(DN_ALPHA * x_ref[...] + y_ref[...], g_ref[...], b_ref[...])
    of_ref[...] = y
    ob_ref[...] = y.astype(BF16)


def pool_layer(x, row0, B, L, pos0, hist, w_pool_b, pool_scale, g, b, carried=None, tr=256):
    tr = _tile(L, tr, POOL_HIST + 1)
    H = POOL_HIST + 1
    nb = L // tr
    assert row0 % tr == 0
    hist_p = jnp.concatenate([jnp.zeros((B, 1, D_MODEL), F32), hist], 1)
    vec = pl.BlockSpec((1, D_MODEL), lambda b_, i: (0, 0))
    row = pl.BlockSpec((tr, D_MODEL), lambda b_, i: (row0 // tr + b_ * nb + i, 0))
    c_specs, c_args, aliases = _carry_specs(7, carried)
    return pl.pallas_call(
        functools.partial(_pool_kernel, tr=tr, pos0=pos0),
        grid=(B, nb),
        in_specs=[row,
                  pl.BlockSpec((H, D_MODEL),
                               lambda b_, i: (jnp.maximum((row0 + b_ * L + i * tr) // H - 1, 0), 0)),
                  pl.BlockSpec((None, H, D_MODEL), lambda b_, i: (b_, 0, 0)),
                  pl.BlockSpec((len(POOL_WINDOWS), POOL_GROUP_DIM, POOL_GROUP_DIM), lambda b_, i: (0, 0, 0)),
                  vec, vec, vec] + c_specs,
        out_specs=[row, row],
        out_shape=[jax.ShapeDtypeStruct(x.shape, F32), jax.ShapeDtypeStruct(x.shape, BF16)],
        input_output_aliases=aliases,
        scratch_shapes=[pltpu.VMEM((tr + H, D_MODEL), F32), pltpu.VMEM((tr, D_MODEL), F32)],
        compiler_params=_cparams(("parallel", "parallel")),
        name="pool_layer",
    )(x, x, hist_p, w_pool_b, pool_scale.reshape(1, -1), g.reshape(1, -1), b.reshape(1, -1), *c_args)


def _conv_kernel(cur_ref, prev_ref, hist_ref, w_ref, o_ref, cat_ref, *, tr, tc, nq, nqk):
    i = pl.program_id(1)
    c = pl.program_id(2)
    cat_ref[0:SUBLANES, :] = jnp.where(i == 0, hist_ref[...], prev_ref[...])
    cat_ref[SUBLANES:, :] = cur_ref[...]
    acc = cur_ref[...] * w_ref[C_CONV - 1:C_CONV, :]
    for jj in range(C_CONV - 1):
        s = SUBLANES - (C_CONV - 1) + jj
        acc = acc + cat_ref[s:s + tr, :] * w_ref[jj:jj + 1, :]
    y = acc * _sigmoid(acc)

    @pl.when(c < nqk)
    def _():
        scale = jnp.where(c < nq, C_HEAD_DIM ** -0.5, 1.0)
        for h in range(tc // C_HEAD_DIM):
            sl = slice(h * C_HEAD_DIM, (h + 1) * C_HEAD_DIM)
            yh = y[:, sl]
            o_ref[:, sl] = yh * (lax.rsqrt(jnp.sum(yh * yh, axis=1, keepdims=True) + 1e-6) * scale)

    @pl.when(c >= nqk)
    def _():
        o_ref[...] = y


def conv_layer(proj, row0, B, L, hist, conv_w, tr=512, tc=1024):
    tr = _tile(L, tr)
    nb = L // tr
    assert row0 % tr == 0
    hist_p = jnp.concatenate([jnp.zeros((B, SUBLANES - (C_CONV - 1), C_CONV_DIM), F32), hist], 1)
    return pl.pallas_call(
        functools.partial(_conv_kernel, tr=tr, tc=tc, nq=C_KEY_DIM // tc, nqk=2 * C_KEY_DIM // tc),
        grid=(B, nb, C_CONV_DIM // tc),
        in_specs=[pl.BlockSpec((tr, tc), lambda b, i, c: (row0 // tr + b * nb + i, c)),
                  pl.BlockSpec((SUBLANES, tc),
                               lambda b, i, c: (jnp.maximum((row0 + b * L + i * tr) // SUBLANES - 1, 0), c)),
                  pl.BlockSpec((None, SUBLANES, tc), lambda b, i, c: (b, 0, c)),
                  pl.BlockSpec((C_CONV, tc), lambda b, i, c: (0, c))],
        out_specs=pl.BlockSpec((tr, tc), lambda b, i, c: (b * nb + i, c)),
        out_shape=jax.ShapeDtypeStruct((B * L, C_CONV_DIM), F32),
        scratch_shapes=[pltpu.VMEM((tr + SUBLANES, tc), F32)],
        compiler_params=_cparams(("parallel", "parallel", "parallel")),
        name="conv_layer",
    )(proj, proj, hist_p, conv_w)


def _gate_kernel(ba_ref, al_ref, dt_ref, o_ref):
    ba = ba_ref[...]
    lane = lax.broadcasted_iota(jnp.int32, ba.shape, 1)
    x = ba + dt_ref[...]
    softplus = jnp.maximum(x, 0.0) + jnp.log(1.0 + jnp.exp(-jnp.abs(x)))
    g = jnp.where(jnp.logical_and(lane >= C_V_HEADS, lane < 2 * C_V_HEADS),
                  -jnp.exp(al_ref[...]) * softplus, 0.0)
    r = lax.broadcasted_iota(jnp.int32, (CHUNK, CHUNK), 0)
    c = lax.broadcasted_iota(jnp.int32, (CHUNK, CHUNK), 1)
    tril = jnp.where(r >= c, 1.0, 0.0).astype(F32)
    gc = _dot3(tril, g)
    o_ref[...] = jnp.where(lane < C_V_HEADS, _sigmoid(ba), gc)


def gate_layer(ba, a_log, dt_bias):
    T = ba.shape[0]
    pad = lambda v: jnp.zeros((1, LANES), F32).at[0, C_V_HEADS:2 * C_V_HEADS].set(v.astype(F32))
    vec = pl.BlockSpec((1, LANES), lambda i: (0, 0))
    blk = pl.BlockSpec((CHUNK, LANES), lambda i: (i, 0))
    return pl.pallas_call(
        _gate_kernel, grid=(T // CHUNK,), in_specs=[blk, vec, vec], out_specs=blk,
        out_shape=jax.ShapeDtypeStruct((T, LANES), F32),
        compiler_params=_cparams(("parallel",)), name="gate_layer",
    )(ba, pad(a_log), pad(dt_bias))


def _delta_kernel(q_ref, k_ref, v_ref, z_ref, col_ref, row_ref, s0_ref, ng_ref, *rest, nc):
    o_ref, sout_ref, S_ref = rest[-3:]
    c = pl.program_id(2)

    @pl.when(c == 0)
    def _():
        S_ref[...] = s0_ref[...]

    col = col_ref[...]
    rowg = row_ref[...]
    r = lax.broadcasted_iota(jnp.int32, (CHUNK, CHUNK), 0)
    cc = lax.broadcasted_iota(jnp.int32, (CHUNK, CHUNK), 1)
    incl, strict = r >= cc, r > cc
    eye = jnp.where(r == cc, 1.0, 0.0).astype(F32)
    rep = C_V_HEADS // C_QK_HEADS
    heads = range(HEAD_GROUP)
    hsl = lambda h: slice(h * C_HEAD_DIM, (h + 1) * C_HEAD_DIM)
    kh = [k_ref[:, hsl(h)] for h in range(HEAD_GROUP // rep)]
    khb = [k.astype(BF16) for k in kh]
    gcc = [col[:, h:h + 1] for h in heads]
    beta = [col[:, HEAD_GROUP + h:HEAD_GROUP + h + 1] for h in heads]
    decay = [jnp.where(incl, jnp.exp(jnp.where(incl, gcc[h] - rowg[h:h + 1, :], 0.0)), 0.0) for h in heads]
    kb = [kh[h // rep] * beta[h] for h in heads]
    kk = [_dot_nt(kb[h].astype(BF16), khb[h // rep]) for h in heads]
    qk = [_dot_nt(q_ref[:, hsl(h // rep)].astype(BF16), khb[h // rep]) for h in heads]
    npow = [jnp.where(strict, -kk[h] * decay[h], 0.0) for h in heads]
    t_inv = [eye + npow[h] for h in heads]
    for _ in range(int(math.log2(CHUNK)) - 1):
        npow = [_dot3(n, n) for n in npow]
        t_inv = [t_inv[h] + _dot3(t_inv[h], npow[h]) for h in heads]
    tb = [t.astype(BF16) for t in t_inv]
    egc = [jnp.exp(g) for g in gcc]
    u = [_dot(tb[h], (v_ref[:, hsl(h)] * beta[h]).astype(BF16)) for h in heads]
    w = [_dot(tb[h], (kb[h] * egc[h]).astype(BF16)) for h in heads]
    S = [S_ref[h] for h in heads]
    Sb = [s.astype(BF16) for s in S]
    vnb = [(u[h] - _dot(w[h].astype(BF16), Sb[h])).astype(BF16) for h in heads]
    a_qk = [jnp.where(incl, qk[h] * decay[h], 0.0).astype(BF16) for h in heads]
    o = [_dot((q_ref[:, hsl(h // rep)] * egc[h]).astype(BF16), Sb[h]) + _dot(a_qk[h], vnb[h]) for h in heads]
    for h in heads:
        gl = gcc[h][CHUNK - 1:CHUNK, :]
        kg = kh[h // rep] * jnp.exp(gl - gcc[h])
        S_ref[h] = S[h] * jnp.exp(gl) + _dot_tn(kg.astype(BF16), vnb[h])
    for h in heads:
        zz = z_ref[:, hsl(h)]
        ms = jnp.mean(o[h] * o[h], axis=1, keepdims=True)
        y = o[h] * lax.rsqrt(ms + 1e-6) * ng_ref[...] * (zz * _sigmoid(zz))
        o_ref[:, hsl(h)] = y.astype(o_ref.dtype)

    @pl.when(c == nc - 1)
    def _():
        sout_ref[...] = S_ref[...]


def delta_layer(qkvc, proj, gates, row0, B, L, state, norm_g, carried=None):
    nc = L // CHUNK
    ng = C_V_HEADS // HEAD_GROUP
    rows = B * L
    assert L % CHUNK == 0 and row0 % CHUNK == 0
    gt = lax.slice_in_dim(gates, row0, row0 + rows, axis=0)
    beta = gt[:, :C_V_HEADS].reshape(rows, ng, HEAD_GROUP)
    gc = gt[:, C_V_HEADS:2 * C_V_HEADS].reshape(rows, ng, HEAD_GROUP)
    col = jnp.concatenate([gc, beta, jnp.zeros((rows, ng, LANES - 2 * HEAD_GROUP), F32)], -1)
    col = col.transpose(1, 0, 2)
    rowg = gt[:, C_V_HEADS:2 * C_V_HEADS].reshape(B * nc, CHUNK, C_V_HEADS).transpose(0, 2, 1)
    qw = HEAD_GROUP * C_HEAD_DIM * C_QK_HEADS // C_V_HEADS
    vw = HEAD_GROUP * C_HEAD_DIM
    kb0 = C_KEY_DIM // qw
    vb0 = 2 * C_KEY_DIM // vw
    zb0 = C_CONV_DIM // vw
    st_spec = pl.BlockSpec((None, HEAD_GROUP, C_HEAD_DIM, C_HEAD_DIM), lambda b, h, c: (b, h, 0, 0))
    c_specs, c_args, aliases = _carry_specs(8, carried)
    o, s_new = pl.pallas_call(
        functools.partial(_delta_kernel, nc=nc),
        grid=(B, ng, nc),
        in_specs=[pl.BlockSpec((CHUNK, qw), lambda b, h, c: (b * nc + c, h)),
                  pl.BlockSpec((CHUNK, qw), lambda b, h, c: (b * nc + c, kb0 + h)),
                  pl.BlockSpec((CHUNK, vw), lambda b, h, c: (b * nc + c, vb0 + h)),
                  pl.BlockSpec((CHUNK, vw), lambda b, h, c: (row0 // CHUNK + b * nc + c, zb0 + h)),
                  pl.BlockSpec((None, CHUNK, LANES), lambda b, h, c: (h, b * nc + c, 0)),
                  pl.BlockSpec((None, HEAD_GROUP, CHUNK), lambda b, h, c: (b * nc + c, h, 0)),
                  st_spec,
                  pl.BlockSpec((1, C_HEAD_DIM), lambda b, h, c: (0, 0))] + c_specs,
        out_specs=[pl.BlockSpec((CHUNK, vw), lambda b, h, c: (row0 // CHUNK + b * nc + c, h)), st_spec],
        out_shape=[jax.ShapeDtypeStruct((proj.shape[0], C_VAL_DIM), BF16),
                   jax.ShapeDtypeStruct(state.shape, F32)],
        input_output_aliases=aliases,
        scratch_shapes=[pltpu.VMEM((HEAD_GROUP, C_HEAD_DIM, C_HEAD_DIM), F32)],
        compiler_params=_cparams(("parallel", "parallel", "arbitrary")),
        name="delta_layer",
    )(qkvc, qkvc, qkvc, proj, col, rowg, state.astype(F32), norm_g.reshape(1, C_HEAD_DIM), *c_args)
    return o, s_new


def _route_kernel(x_ref, w_ref, b_ref, sel_ref, gate_ref, rank_ref, cnt_ref, carry_ref, *, n_precise):
    i = pl.program_id(0)
    tm = x_ref.shape[0]

    @pl.when(i == 0)
    def _():
        carry_ref[...] = jnp.zeros(carry_ref.shape, F32)

    lane_i = lax.broadcasted_iota(jnp.int32, (tm, LANES), 1)
    lane = lane_i.astype(F32)
    x, w = x_ref[...], w_ref[...]
    logits = jnp.where(i < n_precise, _dot3(x, w), _dot(x.astype(BF16), w.astype(BF16))) + b_ref[...]
    v = jnp.where(lane_i < N_EXPERTS, logits, -jnp.inf)
    oh, vals = [], []
    sel = jnp.zeros((tm, LANES), F32)
    for k in range(TOP_K):
        mk = jnp.max(v, axis=1, keepdims=True)
        ik = jnp.min(jnp.where(v == mk, lane, float(LANES)), axis=1, keepdims=True)
        hit = lane == ik
        oh.append(jnp.where(hit, 1.0, 0.0).astype(F32))
        vals.append(mk)
        sel = jnp.where(lane_i == k, ik, sel)
        v = jnp.where(hit, -jnp.inf, v)
    ex = [jnp.exp(vals[k] - vals[0]) for k in range(TOP_K)]
    den = ex[0]
    for k in range(1, TOP_K):
        den = den + ex[k]
    gate = jnp.zeros((tm, LANES), F32)
    for k in range(TOP_K):
        gate = jnp.where(lane_i == k, ex[k] / den, gate)
    tot = oh[0]
    for k in range(1, TOP_K):
        tot = tot + oh[k]
    r = lax.broadcasted_iota(jnp.int32, (tm, tm), 0)
    c = lax.broadcasted_iota(jnp.int32, (tm, tm), 1)
    before = _dot(jnp.where(r > c, 1.0, 0.0).astype(BF16), tot.astype(BF16)) + carry_ref[...]
    rank = jnp.zeros((tm, LANES), F32)
    for k in range(TOP_K):
        rank = jnp.where(lane_i == k, jnp.sum(oh[k] * before, axis=1, keepdims=True), rank)
        before = before + oh[k]
    sel_ref[...] = sel.astype(jnp.int32)
    gate_ref[...] = gate
    rank_ref[...] = rank.astype(jnp.int32)
    carry_ref[...] += jnp.sum(tot, axis=0, keepdims=True)
    cnt_ref[...] = carry_ref[...].astype(jnp.int32)


def route(x, w_router, b_router, precise_rows, tm=512):
    T = x.shape[0]
    tm = _tile(T, tm)
    w = jnp.zeros((D_MODEL, LANES), F32).at[:, :N_EXPERTS].set(w_router)
    b = jnp.zeros((1, LANES), F32).at[0, :N_EXPERTS].set(b_router)
    blk = pl.BlockSpec((tm, LANES), lambda i: (i, 0))
    vec = pl.BlockSpec((1, LANES), lambda i: (0, 0))
    sel, gate, rank, cnt = pl.pallas_call(
        functools.partial(_route_kernel, n_precise=precise_rows // tm), grid=(T // tm,),
        in_specs=[pl.BlockSpec((tm, D_MODEL), lambda i: (i, 0)),
                  pl.BlockSpec((D_MODEL, LANES), lambda i: (0, 0)), vec],
        out_specs=[blk, blk, blk, vec],
        out_shape=[jax.ShapeDtypeStruct((T, LANES), jnp.int32), jax.ShapeDtypeStruct((T, LANES), F32),
                   jax.ShapeDtypeStruct((T, LANES), jnp.int32), jax.ShapeDtypeStruct((1, LANES), jnp.int32)],
        scratch_shapes=[pltpu.VMEM((1, LANES), F32)],
        compiler_params=_cparams(("arbitrary",)), name="route",
    )(x, w, b)
    return sel[:, :TOP_K], gate[:, :TOP_K], rank[:, :TOP_K], cnt[0, :N_EXPERTS]


def _gmm1_kernel(be_ref, first_ref, nact_ref, x_ref, wg_ref, wu_ref, bg_ref, bu_ref, o_ref,
                 wgb_ref, wub_ref):
    r = pl.program_id(1)
    active = r < nact_ref[0]

    @pl.when(jnp.logical_and(active, first_ref[r] == 1))
    def _():
        wgb_ref[...] = wg_ref[...].astype(BF16)
        wub_ref[...] = wu_ref[...].astype(BF16)

    @pl.when(active)
    def _():
        x = x_ref[...].astype(BF16)
        gate = jnp.minimum(_dot(x, wgb_ref[...]) + bg_ref[...], SWIGLU_LIMIT)
        up = jnp.clip(_dot(x, wub_ref[...]) + bu_ref[...], -SWIGLU_LIMIT, SWIGLU_LIMIT)
        act = (up + 1.0) * gate * _sigmoid(SWIGLU_ALPHA * gate)
        o_ref[...] = act.astype(o_ref.dtype)

    @pl.when(jnp.logical_not(active))
    def _():
        o_ref[...] = jnp.zeros(o_ref.shape, o_ref.dtype)


def gmm_gate_up(xs, w_gu, b_gu, layer, blk_exp, first, nact, tn=1024):
    R = xs.shape[0]
    nb = R // MOE_TILE
    nj = D_FF // tn
    rmap = lambda r, na: jnp.minimum(r, na[0] - 1)
    b3 = b_gu.reshape(DEPTH, N_EXPERTS, 1, 2 * D_FF)
    grid_spec = pltpu.PrefetchScalarGridSpec(
        num_scalar_prefetch=3,
        grid=(nj, nb),
        in_specs=[pl.BlockSpec((MOE_TILE, D_MODEL), lambda j, r, be, fi, na: (rmap(r, na), 0)),
                  pl.BlockSpec((None, None, D_MODEL, tn), lambda j, r, be, fi, na: (layer, be[r], 0, j)),
                  pl.BlockSpec((None, None, D_MODEL, tn), lambda j, r, be, fi, na: (layer, be[r], 0, nj + j)),
                  pl.BlockSpec((None, None, 1, tn), lambda j, r, be, fi, na: (layer, be[r], 0, j)),
                  pl.BlockSpec((None, None, 1, tn), lambda j, r, be, fi, na: (layer, be[r], 0, nj + j))],
        out_specs=pl.BlockSpec((MOE_TILE, tn), lambda j, r, be, fi, na: (r, j)),
        scratch_shapes=[pltpu.VMEM((D_MODEL, tn), BF16)] * 2)
    return pl.pallas_call(
        _gmm1_kernel, grid_spec=grid_spec,
        out_shape=jax.ShapeDtypeStruct((R, D_FF), BF16),
        compiler_params=_cparams(("arbitrary", "arbitrary")),
        name="gmm_gate_up",
    )(blk_exp, first, nact, xs, w_gu, w_gu, b3, b3)


def _gmm2_kernel(be_ref, first_ref, nact_ref, a_ref, w_ref, b_ref, o_ref, wb_ref):
    r = pl.program_id(1)
    active = r < nact_ref[0]

    @pl.when(jnp.logical_and(active, first_ref[r] == 1))
    def _():
        wb_ref[...] = w_ref[...].astype(BF16)

    @pl.when(active)
    def _():
        o_ref[...] = _dot(a_ref[...], wb_ref[...]) + b_ref[...]

    @pl.when(jnp.logical_not(active))
    def _():
        o_ref[...] = jnp.zeros(o_ref.shape, o_ref.dtype)


def gmm_down(act, w_down, b_down, layer, blk_exp, first, nact, tn=2048):
    R = act.shape[0]
    nb = R // MOE_TILE
    rmap = lambda r, na: jnp.minimum(r, na[0] - 1)
    b3 = b_down.reshape(DEPTH, N_EXPERTS, 1, D_MODEL)
    grid_spec = pltpu.PrefetchScalarGridSpec(
        num_scalar_prefetch=3,
        grid=(D_MODEL // tn, nb),
        in_specs=[pl.BlockSpec((MOE_TILE, D_FF), lambda j, r, be, fi, na: (rmap(r, na), 0)),
                  pl.BlockSpec((None, None, D_FF, tn), lambda j, r, be, fi, na: (layer, be[r], 0, j)),
                  pl.BlockSpec((None, None, 1, tn), lambda j, r, be, fi, na: (layer, be[r], 0, j))],
        out_specs=pl.BlockSpec((MOE_TILE, tn), lambda j, r, be, fi, na: (r, j)),
        scratch_shapes=[pltpu.VMEM((D_FF, tn), BF16)])
    return pl.pallas_call(
        _gmm2_kernel, grid_spec=grid_spec,
        out_shape=jax.ShapeDtypeStruct((R, D_MODEL), F32),
        compiler_params=_cparams(("arbitrary", "arbitrary")),
        name="gmm_down",
    )(blk_exp, first, nact, act, w_down, b3)


def _combine_ln_kernel(ys_ref, gt_ref, x_ref, g_ref, b_ref, of_ref, ob_ref):
    f = ys_ref[0] * gt_ref[:, 0:1]
    for k in range(1, TOP_K):
        f = f + ys_ref[k] * gt_ref[:, k:k + 1]
    y = _layer_norm(DN_ALPHA * x_ref[...] + f, g_ref[...], b_ref[...])
    of_ref[...] = y
    ob_ref[...] = y.astype(BF16)


def combine_ln(ys_tok, gates, x, g, b, tm=256):
    T = x.shape[0]
    tm = _tile(T, tm)
    vec = pl.BlockSpec((1, D_MODEL), lambda i: (0, 0))
    row = pl.BlockSpec((tm, D_MODEL), lambda i: (i, 0))
    return pl.pallas_call(
        _combine_ln_kernel, grid=(T // tm,),
        in_specs=[pl.BlockSpec((TOP_K, tm, D_MODEL), lambda i: (0, i, 0)),
                  pl.BlockSpec((tm, TOP_K), lambda i: (i, 0)),
                  row, vec, vec],
        out_specs=[row, row],
        out_shape=[jax.ShapeDtypeStruct((T, D_MODEL), F32), jax.ShapeDtypeStruct((T, D_MODEL), BF16)],
        compiler_params=_cparams(("parallel",)), name="combine_ln",
    )(ys_tok, gates, x, g.reshape(1, -1), b.reshape(1, -1))


def moe_layer(x, precise_rows, layer, w_router, b_router, w_gu, b_gu, w_down, b_down, g, b):
    T = x.shape[0]
    top_i, gates, rank, counts = route(x, w_router, b_router, precise_rows)
    n_as = T * TOP_K
    padded = (counts + MOE_TILE - 1) // MOE_TILE * MOE_TILE
    pad_end = jnp.cumsum(padded)
    pad_start = pad_end - padded
    pos = jnp.take(pad_start, top_i, mode="clip") + rank
    n_blocks = -(-n_as // MOE_TILE) + N_EXPERTS
    blk_row0 = jnp.arange(n_blocks, dtype=jnp.int32) * MOE_TILE
    blk_exp = jnp.minimum(jnp.sum((pad_end[None, :] <= blk_row0[:, None]).astype(jnp.int32), axis=1),
                          N_EXPERTS - 1).astype(jnp.int32)
    nact = (pad_end[-1] // MOE_TILE).astype(jnp.int32).reshape(1)
    first = jnp.concatenate([jnp.ones((1,), jnp.int32),
                             (blk_exp[1:] != blk_exp[:-1]).astype(jnp.int32)])
    row_tok = jnp.zeros((n_blocks * MOE_TILE,), jnp.int32).at[pos.reshape(n_as)].set(
        jnp.arange(n_as, dtype=jnp.int32) // TOP_K, unique_indices=True)
    xs = jnp.take(x, row_tok, axis=0, mode="clip")
    act = gmm_gate_up(xs, w_gu, b_gu, layer, blk_exp, first, nact)
    ys = gmm_down(act, w_down, b_down, layer, blk_exp, first, nact)
    ys_tok = jnp.take(ys, pos.T, axis=0, mode="clip")
    return combine_ln(ys_tok, gates, x, g, b)


def kernel(x_prompt, x_sample, cache_a_k, cache_a_v, cache_pool, cache_conv, state_delta, cache_mem_k,
           cache_mem_v, mem_prompt, ln_g, ln_b, w_qkv_a, w_o_a, lam_q1, lam_k1, lam_q2, lam_k2, subln_g,
           w_pool, pool_scale, w_in_c, conv_w_c, a_log_c, dt_bias_c, norm_g_c, w_o_c, w_q_m, w_kv_m,
           w_o_m, w_router, b_router, w_gu, b_gu, w_down, b_down):
    Bp, Lp, _ = x_prompt.shape
    Bs, Ls, _ = x_sample.shape
    past = cache_a_k.shape[2]
    Tp, Ts = Bp * Lp, Bs * Ls
    M = mem_prompt.shape[1]

    x = jnp.concatenate([x_prompt.reshape(Tp, D_MODEL), x_sample.reshape(Ts, D_MODEL)], 0)
    xb = x.astype(BF16)
    pos_all = jnp.concatenate([jnp.tile(jnp.arange(Lp, dtype=jnp.int32), Bp),
                               jnp.tile(past + jnp.arange(Ls, dtype=jnp.int32), Bs)])
    tabs = rope_tables(pos_all)

    mem_b = mem_prompt.reshape(Bp * M, D_MODEL).astype(BF16)
    mem_kv = [matmul(mem_b, w_kv_m[i].astype(BF16), tm=Bp * M) for i in range(DEPTH)]
    ck = cache_mem_k.reshape(DEPTH, Bs, M, D_MODEL)
    cv = cache_mem_v.reshape(DEPTH, Bs, M, D_MODEL)
    cak = cache_a_k.reshape(cache_a_k.shape[0], Bs, past, D_MODEL)
    cav = cache_a_v.reshape(cache_a_v.shape[0], Bs, past, D_MODEL)

    new_k, new_v, new_pool, new_conv, new_delta = [], [], [], [], []
    for i in range(DEPTH):
        m, j = i % N_MIXERS, i // N_MIXERS
        g0, b0 = ln_g[i, 0], ln_b[i, 0]
        if m == 0:
            qkv = matmul_rope(xb, w_qkv_a[j].astype(BF16), tabs, 2 * D_MODEL)
            new_k.append(qkv[:, D_MODEL:2 * D_MODEL])
            new_v.append(qkv[:, 2 * D_MODEL:])
            lam_init = 0.8 - 0.6 * math.exp(-0.3 * i)
            lamv = jnp.stack([lam_q1[j], lam_k1[j], lam_q2[j], lam_k2[j]]).astype(F32)
            o = flash_prompt(qkv, Bp, Lp, lamv, subln_g[j], lam_init)
            o = attn_sample(qkv, Tp, Bs, Ls, cak, cav, j, lamv, subln_g[j], lam_init, carried=(o,))
            x, xb = matmul_res_ln(o, w_o_a[j].astype(BF16), x, g0, b0)
        elif m == 1:
            new_pool.append(x)
            wpb = w_pool[j].astype(BF16)
            pooled = pool_layer(x, 0, Bp, Lp, 0, jnp.zeros((Bp, POOL_HIST, D_MODEL), F32),
                                wpb, pool_scale[j], g0, b0)
            x, xb = pool_layer(x, Tp, Bs, Ls, past, cache_pool[j], wpb, pool_scale[j], g0, b0, carried=pooled)
        else:
            n_main = C_CONV_DIM + C_VAL_DIM
            w_in = w_in_c[j]
            proj = matmul(xb, w_in[:, :n_main].astype(BF16))
            w_ba = jnp.zeros((D_MODEL, LANES), BF16).at[:, :2 * C_V_HEADS].set(w_in[:, n_main:].astype(BF16))
            ba = matmul(xb, w_ba, tn=LANES)
            new_conv.append(proj)
            gates = gate_layer(ba, a_log_c[j], dt_bias_c[j])
            o = None
            for (row0, B, L, hist, st) in (
                    (0, Bp, Lp, jnp.zeros((Bp, C_CONV - 1, C_CONV_DIM), F32),
                     jnp.zeros((Bp, C_V_HEADS, C_HEAD_DIM, C_HEAD_DIM), F32)),
                    (Tp, Bs, Ls, cache_conv[j], state_delta[j])):
                qkvc = conv_layer(proj, row0, B, L, hist, conv_w_c[j])
                o, s_new = delta_layer(qkvc, proj, gates, row0, B, L, st, norm_g_c[j],
                                       carried=None if o is None else (o,))
                new_delta.append(s_new)
            x, xb = matmul_res_ln(o, w_o_c[j].astype(BF16), x, g0, b0)

        wq = w_q_m[i].astype(BF16)
        kv = mem_kv[i]
        c = mem_attn(xb, wq, 0, Bp, Lp, kv, kv,
                     pl.BlockSpec((M, D_MODEL), lambda b, r: (b, 0)),
                     pl.BlockSpec((M, D_MODEL), lambda b, r: (b, 1)))
        cache_spec = pl.BlockSpec((None, None, M, D_MODEL), lambda b, r, i=i: (i, b, 0, 0))
        c = mem_attn(xb, wq, Tp, Bs, Ls, ck, cv, cache_spec, cache_spec, carried=(c,))
        x, xb = matmul_res_ln(c, w_o_m[i].astype(BF16), x, ln_g[i, 1], ln_b[i, 1])

        x, xb = moe_layer(x, Tp, i, w_router[i], b_router[i], w_gu, b_gu, w_down, b_down,
                          ln_g[i, 2], ln_b[i, 2])

    def split(t, shape_p, shape_s):
        return t[:Tp].reshape(shape_p), t[Tp:].reshape(shape_s)

    y_prompt, y_sample = split(x, (Bp, Lp, D_MODEL), (Bs, Ls, D_MODEL))
    kp, ks = zip(*[split(t, (Bp, Lp, 2 * A_HEADS, A_HEAD_DIM), (Bs, Ls, 2 * A_HEADS, A_HEAD_DIM))
                   for t in new_k])
    vp, vs = zip(*[split(t, (Bp, Lp, A_HEADS, A_V_DIM), (Bs, Ls, A_HEADS, A_V_DIM)) for t in new_v])
    pp, ps = zip(*[split(t, (Bp, Lp, D_MODEL), (Bs, Ls, D_MODEL)) for t in new_pool])
    cp, cs = zip(*[split(t[:, :C_CONV_DIM], (Bp, Lp, C_CONV_DIM), (Bs, Ls, C_CONV_DIM)) for t in new_conv])
    mk = jnp.stack([kv[:, :D_MODEL].reshape(Bp, M, M_HEADS, M_HEAD_DIM) for kv in mem_kv])
    mv = jnp.stack([kv[:, D_MODEL:].reshape(Bp, M, M_HEADS, M_HEAD_DIM) for kv in mem_kv])
    return (y_prompt, y_sample, jnp.stack(kp), jnp.stack(vp), jnp.stack(ks), jnp.stack(vs),
            jnp.stack([t[:, -POOL_HIST:] for t in pp]), jnp.stack([t[:, -POOL_HIST:] for t in ps]),
            jnp.stack([t[:, -(C_CONV - 1):] for t in cp]), jnp.stack([t[:, -(C_CONV - 1):] for t in cs]),
            jnp.stack(new_delta[0::2]), jnp.stack(new_delta[1::2]), mk, mv)
```
